```python
import math
import jax, jax.numpy as jnp
from jax import lax
import numpy as np

D_MODEL = 1024
BATCH = 1
SEQ = 16384
DEPTH = 4

CHUNK = 64
HEAD_DIM = 64
SB_HEADS = D_MODEL // (4 * HEAD_DIM)
CA_HEADS = D_MODEL // (2 * HEAD_DIM)
SB_WIDTH = SB_HEADS * HEAD_DIM
CA_WIDTH = CA_HEADS * HEAD_DIM
SSM_WIDTH = D_MODEL // 4
SSM_GROUP_CH = 16
SSM_GROUPS = SSM_WIDTH // SSM_GROUP_CH
SSM_STATE = 64
D_MIX = SB_WIDTH + CA_WIDTH + SSM_WIDTH
D_IN_PROJ = 3 * SB_WIDTH + 3 * CA_WIDTH + SSM_WIDTH
Q_BLOCK = 128
KSUB = 64
MASK_LOGIT = -1e4
LOOKBACK = 8
BAND = (LOOKBACK + 1) * CHUNK
REL_CLIP = 128
D_FF = ((8 * D_MODEL // 3 + 127) // 128) * 128
N_EXPERTS = 8
TOP_K = 2
MOE_BLOCK = 512
N_DENSE = (DEPTH + 1) // 2
N_MOE = DEPTH // 2
RMS_EPS = 1e-6
DT_MIN = 1e-3
DT_MAX = 1e-1

kernel_name = 'hybrid_sb_chunkattn_s5_moe_trunk'


def rms_norm(x, g):
    xf = x.astype(jnp.float32)
    y = xf * lax.rsqrt(jnp.mean(xf * xf, axis=-1, keepdims=True) + RMS_EPS)
    return (y * g.astype(jnp.float32)).astype(x.dtype)


def stick_breaking_attention(q, k, v):
    b, s, h, dh = q.shape
    nq = s // Q_BLOCK
    scale = 1.0 / math.sqrt(dh)
    qf = q.astype(jnp.float32)
    kf = k.astype(jnp.float32)
    vf = v.astype(jnp.float32)
    sub = jnp.arange(KSUB)
    later_in_sub = (sub[:, None] > sub[None, :]).astype(jnp.float32)
    outs = []
    for i in range(nq):
        nk = (i + 1) * Q_BLOCK
        nsub = nk // KSUB
        z = jnp.einsum('bqhd,bkhd->bhqk', qf[:, i * Q_BLOCK:(i + 1) * Q_BLOCK], kf[:, :nk]) * scale
        q_pos = i * Q_BLOCK + jnp.arange(Q_BLOCK)
        k_pos = jnp.arange(nk)
        z = jnp.where(k_pos[None, :] < q_pos[:, None], z, MASK_LOGIT)
        log_not = jax.nn.log_sigmoid(-z)
        ln = log_not.reshape(b, h, Q_BLOCK, nsub, KSUB)
        within = jnp.einsum('bhqjc,cd->bhqjd', ln, later_in_sub)
        blk = jnp.arange(nsub)
        later_blk = (blk[:, None] > blk[None, :]).astype(jnp.float32)
        across = jnp.einsum('bhqj,jm->bhqm', ln.sum(-1), later_blk)
        after = (within + across[..., None]).reshape(b, h, Q_BLOCK, nk)
        w = jnp.exp(log_not + z + after)
        outs.append(jnp.einsum('bhqk,bkhd->bqhd', w, vf[:, :nk]))
    out = jnp.concatenate(outs, axis=1)
    return out.reshape(b, s, h * dh).astype(q.dtype)


def chunk_band_attention(q, k, v, rel_bias):
    b, s, h, dh = q.shape
    nc = s // CHUNK
    scale = 1.0 / math.sqrt(dh)
    pad = LOOKBACK * CHUNK
    kp = jnp.pad(k, ((0, 0), (pad, 0), (0, 0), (0, 0))).reshape(b, nc + LOOKBACK, CHUNK, h, dh)
    vp = jnp.pad(v, ((0, 0), (pad, 0), (0, 0), (0, 0))).reshape(b, nc + LOOKBACK, CHUNK, h, dh)
    band_idx = jnp.arange(nc)[:, None] + jnp.arange(LOOKBACK + 1)[None, :]
    kb = kp[:, band_idx].reshape(b, nc, BAND, h, dh)
    vb = vp[:, band_idx].reshape(b, nc, BAND, h, dh)
    qc = q.reshape(b, nc, CHUNK, h, dh)
    scores = jnp.einsum('bcqhd,bckhd->bchqk', qc.astype(jnp.float32), kb.astype(jnp.float32)) * scale
    qi = jnp.arange(CHUNK)[:, None]
    kk = jnp.arange(BAND)[None, :]
    dist = LOOKBACK * CHUNK + qi - kk
    bias_idx = jnp.clip(dist, -REL_CLIP, REL_CLIP) + REL_CLIP
    bias = rel_bias.astype(jnp.float32)[:, bias_idx]
    valid = (jnp.arange(nc)[:, None] - LOOKBACK + (jnp.arange(BAND) // CHUNK)[None, :]) >= 0
    scores = jnp.where(valid[None, :, None, None, :], scores + bias[None, None], -1e30)
    p = jax.nn.softmax(scores, axis=-1)
    out = jnp.einsum('bchqk,bckhd->bcqhd', p, vb.astype(jnp.float32))
    return out.reshape(b, s, h * dh).astype(q.dtype)


def _complex_affine_combine(e1, e2):
    a1r, a1i, b1r, b1i = e1
    a2r, a2i, b2r, b2i = e2
    ar = a2r * a1r - a2i * a1i
    ai = a2r * a1i + a2i * a1r
    br = a2r * b1r - a2i * b1i + b2r
    bi = a2r * b1i + a2i * b1r + b2i
    return (ar, ai, br, bi)


def s5_mixer(u, lam_re, lam_im, log_dt, b_re, b_im, c_re, c_im, d_skip, w_glu):
    bsz, s, w = u.shape
    uf = u.astype(jnp.float32)
    ug = uf.reshape(bsz, s, SSM_GROUPS, SSM_GROUP_CH)
    dt = jnp.exp(log_dt.astype(jnp.float32))[:, None]
    lre = lam_re.astype(jnp.float32)
    lim = lam_im.astype(jnp.float32)
    mag = jnp.exp(lre * dt)
    ang = lim * dt
    abar_re = mag * jnp.cos(ang)
    abar_im = mag * jnp.sin(ang)
    nre = abar_re - 1.0
    nim = abar_im
    den = lre * lre + lim * lim
    zoh_re = ((nre * lre + nim * lim) / den)[..., None]
    zoh_im = ((nim * lre - nre * lim) / den)[..., None]
    bre = b_re.astype(jnp.float32)
    bim = b_im.astype(jnp.float32)
    bbar_re = zoh_re * bre - zoh_im * bim
    bbar_im = zoh_re * bim + zoh_im * bre
    bu_re = jnp.einsum('gpc,bsgc->bsgp', bbar_re, ug)
    bu_im = jnp.einsum('gpc,bsgc->bsgp', bbar_im, ug)
    a_re = jnp.broadcast_to(abar_re, bu_re.shape)
    a_im = jnp.broadcast_to(abar_im, bu_im.shape)
    _, _, x_re, x_im = lax.associative_scan(_complex_affine_combine, (a_re, a_im, bu_re, bu_im), axis=1)
    y = (jnp.einsum('gcp,bsgp->bsgc', c_re.astype(jnp.float32), x_re)
         - jnp.einsum('gcp,bsgp->bsgc', c_im.astype(jnp.float32), x_im))
    y = y.reshape(bsz, s, w) + d_skip.astype(jnp.float32) * uf
    y = jax.nn.gelu(y)
    hg = jnp.einsum('bsc,ce->bse', y, w_glu.astype(jnp.float32))
    out = hg[..., :w] * jax.nn.sigmoid(hg[..., w:])
    return out.astype(u.dtype)


def swiglu(x, w_gate, w_up, w_down):
    hid = jax.nn.silu(x @ w_gate) * (x @ w_up)
    return hid @ w_down


def moe_swiglu(xn, w_router, w_gate, w_up, w_down):
    b, s, d = xn.shape
    t = b * s
    xf = xn.reshape(t, d)
    logits = jnp.einsum('td,de->te', xf.astype(jnp.float32), w_router.astype(jnp.float32))
    top_vals, top_idx = lax.top_k(logits, TOP_K)
    gates = jax.nn.softmax(top_vals, axis=-1)
    flat_e = top_idx.reshape(-1)
    flat_tok = jnp.repeat(jnp.arange(t, dtype=jnp.int32), TOP_K)
    flat_g = gates.reshape(-1)
    order = jnp.argsort(flat_e)
    se = flat_e[order]
    stok = flat_tok[order]
    sg = flat_g[order]
    counts = jnp.bincount(flat_e, length=N_EXPERTS)
    padded = (counts + MOE_BLOCK - 1) // MOE_BLOCK * MOE_BLOCK
    pend = jnp.cumsum(padded)
    pstart = pend - padded
    ustart = jnp.cumsum(counts) - counts
    pos = pstart[se] + (jnp.arange(t * TOP_K) - ustart[se])
    n_blocks = -(-(t * TOP_K) // MOE_BLOCK) + N_EXPERTS
    n_slots = n_blocks * MOE_BLOCK
    slot_tok = jnp.full((n_slots,), t, dtype=jnp.int32).at[pos].set(stok)
    slot_gate = jnp.zeros((n_slots,), jnp.float32).at[pos].set(sg)
    block_e = jnp.minimum(jnp.searchsorted(pend, jnp.arange(n_blocks) * MOE_BLOCK, side='right'), N_EXPERTS - 1)
    xpad = jnp.concatenate([xf, jnp.zeros((1, d), xf.dtype)], axis=0)

    def block_fn(args):
        tok, g, e = args
        xb = xpad[tok]
        yb = swiglu(xb, w_gate[e], w_up[e], w_down[e])
        return yb * g.astype(yb.dtype)[:, None]

    ys = lax.map(block_fn, (slot_tok.reshape(n_blocks, MOE_BLOCK), slot_gate.reshape(n_blocks, MOE_BLOCK), block_e))
    out = jnp.zeros((t + 1, d), ys.dtype).at[slot_tok].add(ys.reshape(n_slots, d))[:t]
    return out.reshape(b, s, d).astype(xn.dtype)


def setup_inputs(seed: int = 0) -> dict:
    key = jax.random.key(seed)
    ks = jax.random.split(key, 32)

    def nrm(k, shape, scale):
        return jax.random.normal(k, shape, jnp.float32) * scale

    def gain(k, shape):
        return 1.0 + 0.05 * jax.random.normal(k, shape, jnp.float32)

    n_idx = jnp.arange(SSM_STATE, dtype=jnp.float32)
    return {
        'x': nrm(ks[0], (BATCH, SEQ, D_MODEL), 1.0),
        'norm_mix': gain(ks[1], (DEPTH, D_MODEL)),
        'w_in': nrm(ks[2], (DEPTH, D_MODEL, D_IN_PROJ), D_MODEL ** -0.5),
        'q_norm_sb': gain(ks[3], (DEPTH, HEAD_DIM)),
        'k_norm_sb': gain(ks[4], (DEPTH, HEAD_DIM)),
        'q_norm_ca': gain(ks[5], (DEPTH, HEAD_DIM)),
        'k_norm_ca': gain(ks[6], (DEPTH, HEAD_DIM)),
        'rel_bias': nrm(ks[7], (DEPTH, CA_HEADS, 2 * REL_CLIP + 1), 0.3),
        'lam_re': -0.5 + nrm(ks[8], (DEPTH, SSM_GROUPS, SSM_STATE), 0.01),
        'lam_im': math.pi * n_idx + nrm(ks[9], (DEPTH, SSM_GROUPS, SSM_STATE), 0.01),
        'log_dt': jax.random.uniform(ks[10], (DEPTH, SSM_GROUPS), jnp.float32, math.log(DT_MIN), math.log(DT_MAX)),
        'b_re': nrm(ks[11], (DEPTH, SSM_GROUPS, SSM_STATE, SSM_GROUP_CH), (2 * SSM_GROUP_CH) ** -0.5),
        'b_im': nrm(ks[12], (DEPTH, SSM_GROUPS, SSM_STATE, SSM_GROUP_CH), (2 * SSM_GROUP_CH) ** -0.5),
        'c_re': nrm(ks[13], (DEPTH, SSM_GROUPS, SSM_GROUP_CH, SSM_STATE), SSM_STATE ** -0.5),
        'c_im': nrm(ks[14], (DEPTH, SSM_GROUPS, SSM_GROUP_CH, SSM_STATE), SSM_STATE ** -0.5),
        'd_skip': nrm(ks[15], (DEPTH, SSM_WIDTH), 0.5),
        'w_glu': nrm(ks[16], (DEPTH, SSM_WIDTH, 2 * SSM_WIDTH), SSM_WIDTH ** -0.5),
        'g_out': gain(ks[17], (DEPTH, D_MIX)),
        'w_out': nrm(ks[18], (DEPTH, D_MIX, D_MODEL), 0.5 * D_MIX ** -0.5),
        'norm_ffn': gain(ks[19], (DEPTH, D_MODEL)),
        'w_gate_dense': nrm(ks[20], (N_DENSE, D_MODEL, D_FF), D_MODEL ** -0.5),
        'w_up_dense': nrm(ks[21], (N_DENSE, D_MODEL, D_FF), D_MODEL ** -0.5),
        'w_down_dense': nrm(ks[22], (N_DENSE, D_FF, D_MODEL), 0.5 * D_FF ** -0.5),
        'w_router': nrm(ks[23], (N_MOE, D_MODEL, N_EXPERTS), D_MODEL ** -0.5),
        'w_gate_moe': nrm(ks[24], (N_MOE, N_EXPERTS, D_MODEL, D_FF), D_MODEL ** -0.5),
        'w_up_moe': nrm(ks[25], (N_MOE, N_EXPERTS, D_MODEL, D_FF), D_MODEL ** -0.5),
        'w_down_moe': nrm(ks[26], (N_MOE, N_EXPERTS, D_FF, D_MODEL), 0.5 * D_FF ** -0.5),
    }


def reference(x, norm_mix, w_in, q_norm_sb, k_norm_sb, q_norm_ca, k_norm_ca, rel_bias,
              lam_re, lam_im, log_dt, b_re, b_im, c_re, c_im, d_skip, w_glu, g_out, w_out,
              norm_ffn, w_gate_dense, w_up_dense, w_down_dense, w_router, w_gate_moe,
              w_up_moe, w_down_moe):
    b, s, _ = x.shape
    o1 = SB_WIDTH
    o2 = 2 * SB_WIDTH
    o3 = 3 * SB_WIDTH
    o4 = o3 + CA_WIDTH
    o5 = o4 + CA_WIDTH
    o6 = o5 + CA_WIDTH
    for l in range(DEPTH):
        h = rms_norm(x, norm_mix[l])
        proj = jnp.einsum('bsd,de->bse', h, w_in[l])
        q_sb = rms_norm(proj[..., :o1].reshape(b, s, SB_HEADS, HEAD_DIM), q_norm_sb[l])
        k_sb = rms_norm(proj[..., o1:o2].reshape(b, s, SB_HEADS, HEAD_DIM), k_norm_sb[l])
        v_sb = proj[..., o2:o3].reshape(b, s, SB_HEADS, HEAD_DIM)
        q_ca = rms_norm(proj[..., o3:o4].reshape(b, s, CA_HEADS, HEAD_DIM), q_norm_ca[l])
        k_ca = rms_norm(proj[..., o4:o5].reshape(b, s, CA_HEADS, HEAD_DIM), k_norm_ca[l])
        v_ca = proj[..., o5:o6].reshape(b, s, CA_HEADS, HEAD_DIM)
        u = proj[..., o6:]
        o_sb = stick_breaking_attention(q_sb, k_sb, v_sb)
        o_ca = chunk_band_attention(q_ca, k_ca, v_ca, rel_bias[l])
        o_ssm = s5_mixer(u, lam_re[l], lam_im[l], log_dt[l], b_re[l], b_im[l], c_re[l], c_im[l], d_skip[l], w_glu[l])
        g = g_out[l]
        mixed = jnp.concatenate([
            rms_norm(o_sb, g[:SB_WIDTH]),
            rms_norm(o_ca, g[SB_WIDTH:SB_WIDTH + CA_WIDTH]),
            rms_norm(o_ssm, g[SB_WIDTH + CA_WIDTH:]),
        ], axis=-1)
        x = x + jnp.einsum('bse,ed->bsd', mixed, w_out[l])
        h2 = rms_norm(x, norm_ffn[l])
        i = l // 2
        if l % 2 == 0:
            x = x + swiglu(h2, w_gate_dense[i], w_up_dense[i], w_down_dense[i])
        else:
            x = x + moe_swiglu(h2, w_router[i], w_gate_moe[i], w_up_moe[i], w_down_moe[i])
    return x
```

```python
import functools
import math

import jax
import jax.numpy as jnp
from jax import lax
from jax.experimental import pallas as pl
from jax.experimental.pallas import tpu as pltpu

F32 = jnp.float32
BF16 = jnp.bfloat16

D_MODEL = 1024
DEPTH = 4
CHUNK = 64
HEAD_DIM = 64
SB_WIDTH = 256
CA_WIDTH = 512
CA_HEADS = 8
SSM_WIDTH = 256
SSM_GROUP_CH = 16
SSM_GROUPS = 16
SSM_STATE = 64
N_STATES = SSM_GROUPS * SSM_STATE
D_IN_PROJ = 3 * SB_WIDTH + 3 * CA_WIDTH + SSM_WIDTH
MASK_LOGIT = -1e4
LOOKBACK = 8
REL_CLIP = 128
D_FF = 2816
N_EXPERTS = 8
TOP_K = 2
RMS_EPS = 1e-6

LANES = 128
SUBLANES = 8
VMEM_LIMIT = 56 * 1024 * 1024

TM = 512
SB_BLOCK = 256
CA_BLOCK = 256
SCAN_TILE = 256
FF_TILE = 1408
MOE_BLOCK = 512
SB_UNDERFLOW = 104.0


def _cparams(sem):
    return pltpu.CompilerParams(dimension_semantics=sem, vmem_limit_bytes=VMEM_LIMIT)


def _split_bf16(x):
    hi = x.astype(BF16)
    lo = (x - hi.astype(F32)).astype(BF16)
    return hi, lo


def _rms(x, gain):
    ms = jnp.mean(x * x, axis=-1, keepdims=True)
    return x * lax.rsqrt(ms + RMS_EPS) * gain


def _head_rms(p, gmat, gain):
    hi, lo = _split_bf16(p * p)
    ss = jnp.dot(hi, gmat, preferred_element_type=F32) + jnp.dot(lo, gmat, preferred_element_type=F32)
    return p * lax.rsqrt(ss * (1.0 / HEAD_DIM) + RMS_EPS) * gain


def _in_proj_kernel(x_ref, g_ref, w_ref, gq_sb, gk_sb, gq_ca, gk_ca, gm_ref,
                    qsb_o, ksb_o, vsb_o, qca_o, kca_o, vca_o, u_o):
    h = _rms(x_ref[...], g_ref[...])
    proj = jnp.dot(h.astype(BF16), w_ref[...], preferred_element_type=F32)
    gm_sb = gm_ref[0:SB_WIDTH, 0:SB_WIDTH]
    gm_ca = gm_ref[...]
    o1, o2, o3 = SB_WIDTH, 2 * SB_WIDTH, 3 * SB_WIDTH
    o4, o5, o6 = o3 + CA_WIDTH, o3 + 2 * CA_WIDTH, o3 + 3 * CA_WIDTH
    qsb_o[...] = _head_rms(proj[:, :o1], gm_sb, gq_sb[...]).astype(BF16)
    ksb_o[...] = _head_rms(proj[:, o1:o2], gm_sb, gk_sb[...]).astype(BF16)
    vsb_o[...] = proj[:, o2:o3].astype(BF16)
    qca_o[...] = _head_rms(proj[:, o3:o4], gm_ca, gq_ca[...]).astype(BF16)
    kca_o[...] = _head_rms(proj[:, o4:o5], gm_ca, gk_ca[...]).astype(BF16)
    vca_o[...] = proj[:, o5:o6].astype(BF16)
    u_o[...] = proj[:, o6:]


def _in_proj(x, g, w, gq_sb, gk_sb, gq_ca, gk_ca, gm):
    s = x.shape[0]
    row = lambda w_: pl.BlockSpec((TM, w_), lambda i: (i, 0))
    full = lambda a: pl.BlockSpec(a.shape, lambda i: (0,) * a.ndim)
    return pl.pallas_call(
        _in_proj_kernel,
        grid=(s // TM,),
        in_specs=[row(D_MODEL)] + [full(a) for a in (g, w, gq_sb, gk_sb, gq_ca, gk_ca, gm)],
        out_specs=[row(SB_WIDTH)] * 3 + [row(CA_WIDTH)] * 3 + [row(SSM_WIDTH)],
        out_shape=[jax.ShapeDtypeStruct((s, SB_WIDTH), BF16)] * 3
        + [jax.ShapeDtypeStruct((s, CA_WIDTH), BF16)] * 3
        + [jax.ShapeDtypeStruct((s, SSM_WIDTH), F32)],
        compiler_params=_cparams(("parallel",)),
        name="in_proj",
    )(x, g, w, gq_sb, gk_sb, gq_ca, gk_ca, gm)


def _sb_kernel(q_ref, k_ref, v_ref, tri_ref, o_ref):
    i = pl.program_id(0)
    tri = tri_ref[...]
    row = lax.broadcasted_iota(jnp.int32, (SB_BLOCK, SB_BLOCK), 0)
    col = lax.broadcasted_iota(jnp.int32, (SB_BLOCK, SB_BLOCK), 1)
    rel = col - row
    lane = lax.broadcasted_iota(jnp.int32, (SB_BLOCK, LANES), 1)
    for pair in range(SB_WIDTH // LANES):
        lanes = slice(pair * LANES, (pair + 1) * LANES)
        q_pair = q_ref[:, lanes].astype(F32)
        out_pair = jnp.zeros((SB_BLOCK, LANES), F32)
        for hh in range(LANES // HEAD_DIM):
            in_head = (lane >= hh * HEAD_DIM) & (lane < (hh + 1) * HEAD_DIM)
            qm = jnp.where(in_head, q_pair, 0.0).astype(BF16)

            def cond(c):
                j, done, _, _ = c
                return jnp.logical_and(j >= 0, jnp.logical_not(done))

            def body(c):
                j, _, carry, acc = c
                start = pl.multiple_of(j * SB_BLOCK, SB_BLOCK)
                kb = k_ref[pl.ds(start, SB_BLOCK), lanes]
                vb = v_ref[pl.ds(start, SB_BLOCK), lanes]
                z = lax.dot_general(qm, kb, (((1,), (1,)), ((), ())), preferred_element_type=F32)
                z = jnp.where(rel < (i - j) * SB_BLOCK, z, MASK_LOGIT)
                sp = jnp.maximum(z, 0.0) + jnp.log(1.0 + jnp.exp(-jnp.abs(z)))
                hi, lo = _split_bf16(sp)
                later = (jnp.dot(hi, tri, preferred_element_type=F32)
                         + jnp.dot(lo, tri, preferred_element_type=F32))
                w = jnp.exp(z - sp - (later + carry))
                pv = jnp.dot(w.astype(BF16), vb, preferred_element_type=F32)
                acc = acc + jnp.where(in_head, pv, 0.0)
                carry = carry + jnp.sum(sp, axis=-1, keepdims=True)
                done = jnp.min(carry) > SB_UNDERFLOW
                return j - 1, done, carry, acc

            init = (i, False, jnp.zeros((SB_BLOCK, 1), F32), jnp.zeros((SB_BLOCK, LANES), F32))
            out_pair = out_pair + lax.while_loop(cond, body, init)[3]
        o_ref[:, lanes] = out_pair


def _sb_attention(q, k, v, tri):
    s = q.shape[0]
    whole = pl.BlockSpec(memory_space=pltpu.VMEM)
    return pl.pallas_call(
        _sb_kernel,
        grid=(s // SB_BLOCK,),
        in_specs=[pl.BlockSpec((SB_BLOCK, SB_WIDTH), lambda i: (i, 0)), whole, whole, whole],
        out_specs=pl.BlockSpec((SB_BLOCK, SB_WIDTH), lambda i: (i, 0)),
        out_shape=jax.ShapeDtypeStruct((s, SB_WIDTH), F32),
        compiler_params=_cparams(("parallel",)),
        name="sb_attention",
    )(q, k, v, tri)


CA_WINDOW = 3


def _ca_kernel(q_ref, k0, k1, k2, v0, v1, v2, bm_ref, o_ref):
    i = pl.program_id(0)
    k_refs = (k0, k1, k2)
    v_refs = (v0, v1, v2)
    lane = lax.broadcasted_iota(jnp.int32, (CA_BLOCK, LANES), 1)
    for pair in range(CA_WIDTH // LANES):
        lanes = slice(pair * LANES, (pair + 1) * LANES)
        q_pair = q_ref[:, lanes].astype(F32)
        out_pair = jnp.zeros((CA_BLOCK, LANES), F32)
        for hh in range(LANES // HEAD_DIM):
            h = pair * (LANES // HEAD_DIM) + hh
            in_head = (lane >= hh * HEAD_DIM) & (lane < (hh + 1) * HEAD_DIM)
            qm = jnp.where(in_head, q_pair, 0.0).astype(BF16)
            scores = []
            for jj in range(CA_WINDOW):
                sc = lax.dot_general(qm, k_refs[jj][:, lanes], (((1,), (1,)), ((), ())),
                                     preferred_element_type=F32)
                sc = sc + bm_ref[h, :, jj * CA_BLOCK:(jj + 1) * CA_BLOCK]
                scores.append(jnp.where(i + jj >= CA_WINDOW - 1, sc, -1e30))
            m = functools.reduce(jnp.maximum, [jnp.max(sc, axis=-1, keepdims=True) for sc in scores])
            denom = jnp.zeros((CA_BLOCK, 1), F32)
            pv = jnp.zeros((CA_BLOCK, LANES), F32)
            for jj in range(CA_WINDOW):
                p = jnp.exp(scores[jj] - m)
                denom = denom + jnp.sum(p, axis=-1, keepdims=True)
                pv = pv + jnp.dot(p.astype(BF16), v_refs[jj][:, lanes], preferred_element_type=F32)
            out_pair = out_pair + jnp.where(in_head, pv / denom, 0.0)
        o_ref[:, lanes] = out_pair


def _ca_attention(q, k, v, bm):
    s = q.shape[0]
    blk = lambda off: pl.BlockSpec((CA_BLOCK, CA_WIDTH), lambda i: (jnp.maximum(i - off, 0), 0))
    return pl.pallas_call(
        _ca_kernel,
        grid=(s // CA_BLOCK,),
        in_specs=[blk(0), blk(2), blk(1), blk(0), blk(2), blk(1), blk(0),
                  pl.BlockSpec(bm.shape, lambda i: (0, 0, 0))],
        out_specs=blk(0),
        out_shape=jax.ShapeDtypeStruct((s, CA_WIDTH), F32),
        compiler_params=_cparams(("parallel",)),
        name="ca_attention",
    )(q, k, k, k, v, v, v, bm)


def _ca_bias_mask(rel_bias):
    r = jnp.arange(CA_BLOCK)[:, None]
    c = jnp.arange(CA_WINDOW * CA_BLOCK)[None, :]
    dist = r - c + LOOKBACK * CHUNK
    idx = jnp.clip(dist, -REL_CLIP, REL_CLIP) + REL_CLIP
    back = r // CHUNK - (c // CHUNK - LOOKBACK)
    valid = (back >= 0) & (back <= LOOKBACK)
    return jnp.where(valid[None], rel_bias.astype(F32)[:, idx], -1e30)


def _s5_disc_kernel(lre_ref, lim_ref, ldt_ref, bre_ref, bim_ref, are_o, aim_o, bbre_o, bbim_o):
    lre = lre_ref[...]
    lim = lim_ref[...]
    dt = jnp.exp(ldt_ref[...])
    mag = jnp.exp(lre * dt)
    ang = lim * dt
    a_re = mag * jnp.cos(ang)
    a_im = mag * jnp.sin(ang)
    are_o[...] = a_re
    aim_o[...] = a_im
    nre = a_re - 1.0
    nim = a_im
    den = lre * lre + lim * lim
    zoh_re = (nre * lre + nim * lim) / den
    zoh_im = (nim * lre - nre * lim) / den
    bre = bre_ref[...]
    bim = bim_ref[...]
    bbre_o[...] = zoh_re * bre - zoh_im * bim
    bbim_o[...] = zoh_re * bim + zoh_im * bre


def _s5_discretize(lam_re, lam_im, log_dt, b_re, b_im):
    col = lambda a: a.reshape(N_STATES, 1).astype(F32)
    ldt = jnp.repeat(log_dt.astype(F32), SSM_STATE).reshape(N_STATES, 1)
    b2 = lambda a: a.reshape(N_STATES, SSM_GROUP_CH).astype(F32)
    return pl.pallas_call(
        _s5_disc_kernel,
        out_shape=[jax.ShapeDtypeStruct((N_STATES, 1), F32)] * 2
        + [jax.ShapeDtypeStruct((N_STATES, SSM_GROUP_CH), F32)] * 2,
        name="s5_discretize",
    )(col(lam_re), col(lam_im), ldt, b2(b_re), b2(b_im))


def _s5_in_kernel(u_ref, wre_ref, wim_ref, bre_o, bim_o):
    u = u_ref[...].astype(BF16)
    bre_o[...] = jnp.dot(u, wre_ref[...], preferred_element_type=F32)
    bim_o[...] = jnp.dot(u, wim_ref[...], preferred_element_type=F32)


def _s5_in(u, w_re, w_im):
    s = u.shape[0]
    full = lambda a: pl.BlockSpec(a.shape, lambda i: (0, 0))
    return pl.pallas_call(
        _s5_in_kernel,
        grid=(s // TM,),
        in_specs=[pl.BlockSpec((TM, SSM_WIDTH), lambda i: (i, 0)), full(w_re), full(w_im)],
        out_specs=[pl.BlockSpec((TM, N_STATES), lambda i: (i, 0))] * 2,
        out_shape=[jax.ShapeDtypeStruct((s, N_STATES), F32)] * 2,
        compiler_params=_cparams(("parallel",)),
        name="s5_in",
    )(u, w_re, w_im)


def _s5_scan_kernel(are_ref, aim_ref, bre_ref, bim_ref, xre_o, xim_o, st_ref):
    @pl.when(pl.program_id(0) == 0)
    def _():
        st_ref[...] = jnp.zeros_like(st_ref)

    a_re = are_ref[...]
    a_im = aim_ref[...]

    def step(t, c):
        x_re, x_im = c
        n_re = a_re * x_re - a_im * x_im + bre_ref[t]
        n_im = a_re * x_im + a_im * x_re + bim_ref[t]
        xre_o[t] = n_re
        xim_o[t] = n_im
        return n_re, n_im

    x_re, x_im = lax.fori_loop(0, SCAN_TILE, step, (st_ref[0], st_ref[1]), unroll=8)
    st_ref[0] = x_re
    st_ref[1] = x_im


def _s5_scan(a_re, a_im, bu_re, bu_im):
    s = bu_re.shape[0]
    shape3 = (s, SUBLANES, LANES)
    blk = pl.BlockSpec((SCAN_TILE, SUBLANES, LANES), lambda i: (i, 0, 0))
    small = pl.BlockSpec((SUBLANES, LANES), lambda i: (0, 0))
    x_re, x_im = pl.pallas_call(
        _s5_scan_kernel,
        grid=(s // SCAN_TILE,),
        in_specs=[small, small, blk, blk],
        out_specs=[blk, blk],
        out_shape=[jax.ShapeDtypeStruct(shape3, F32)] * 2,
        scratch_shapes=[pltpu.VMEM((2, SUBLANES, LANES), F32)],
        compiler_params=_cparams(("arbitrary",)),
        name="s5_scan",
    )(a_re.reshape(SUBLANES, LANES), a_im.reshape(SUBLANES, LANES),
      bu_re.reshape(shape3), bu_im.reshape(shape3))
    return x_re.reshape(s, N_STATES), x_im.reshape(s, N_STATES)


def _s5_out_kernel(xre_ref, xim_ref, u_ref, cre_ref, cim_ref, d_ref, wglu_ref, o_ref):
    y = (jnp.dot(xre_ref[...].astype(BF16), cre_ref[...], preferred_element_type=F32)
         - jnp.dot(xim_ref[...].astype(BF16), cim_ref[...], preferred_element_type=F32))
    y = y + d_ref[...] * u_ref[...]
    y = 0.5 * y * (1.0 + jnp.tanh(math.sqrt(2.0 / math.pi) * (y + 0.044715 * (y * y * y))))
    hg = jnp.dot(y.astype(BF16), wglu_ref[...], preferred_element_type=F32)
    o_ref[...] = hg[:, :SSM_WIDTH] / (1.0 + jnp.exp(-hg[:, SSM_WIDTH:]))


def _s5_out(x_re, x_im, u, c_re, c_im, d, w_glu):
    s = u.shape[0]
    full = lambda a: pl.BlockSpec(a.shape, lambda i: (0, 0))
    row = lambda w_: pl.BlockSpec((TM, w_), lambda i: (i, 0))
    return pl.pallas_call(
        _s5_out_kernel,
        grid=(s // TM,),
        in_specs=[row(N_STATES), row(N_STATES), row(SSM_WIDTH)] + [full(a) for a in (c_re, c_im, d, w_glu)],
        out_specs=row(SSM_WIDTH),
        out_shape=jax.ShapeDtypeStruct((s, SSM_WIDTH), F32),
        compiler_params=_cparams(("parallel",)),
        name="s5_out",
    )(x_re, x_im, u, c_re, c_im, d, w_glu)


def _s5_mixer(u, lam_re, lam_im, log_dt, b_re, b_im, c_re, c_im, d_skip, w_glu):
    a_re, a_im, bb_re, bb_im = _s5_discretize(lam_re, lam_im, log_dt, b_re, b_im)
    eye = jnp.eye(SSM_GROUPS, dtype=F32)

    def in_matrix(bb):
        bb = bb.reshape(SSM_GROUPS, SSM_STATE, SSM_GROUP_CH)
        return jnp.einsum('gpc,gh->gchp', bb, eye).reshape(SSM_WIDTH, N_STATES).astype(BF16)

    def out_matrix(c):
        return jnp.einsum('gcp,gh->gphc', c.astype(F32), eye).reshape(N_STATES, SSM_WIDTH).astype(BF16)

    bu_re, bu_im = _s5_in(u, in_matrix(bb_re), in_matrix(bb_im))
    x_re, x_im = _s5_scan(a_re, a_im, bu_re, bu_im)
    return _s5_out(x_re, x_im, u, out_matrix(c_re), out_matrix(c_im),
                   d_skip.reshape(1, SSM_WIDTH).astype(F32), w_glu.astype(BF16))


def _mix_core(osb_ref, oca_ref, ossm_ref, x_ref, g_ref, w_ref, nf_ref):
    g = g_ref[...]
    mixed = jnp.concatenate([
        _rms(osb_ref[...], g[:, :SB_WIDTH]),
        _rms(oca_ref[...], g[:, SB_WIDTH:SB_WIDTH + CA_WIDTH]),
        _rms(ossm_ref[...], g[:, SB_WIDTH + CA_WIDTH:]),
    ], axis=-1)
    x1 = x_ref[...] + jnp.dot(mixed.astype(BF16), w_ref[...], preferred_element_type=F32)
    return x1, _rms(x1, nf_ref[...])


def _mix_dense_kernel(osb_ref, oca_ref, ossm_ref, x_ref, g_ref, w_ref, nf_ref, x1_o, h2_o):
    x1, h2 = _mix_core(osb_ref, oca_ref, ossm_ref, x_ref, g_ref, w_ref, nf_ref)
    x1_o[...] = x1
    h2_o[...] = h2.astype(BF16)


def _mix_moe_kernel(osb_ref, oca_ref, ossm_ref, x_ref, g_ref, w_ref, nf_ref, rhi_ref, rlo_ref,
                    x1_o, h2_o, route_o):
    x1, h2 = _mix_core(osb_ref, oca_ref, ossm_ref, x_ref, g_ref, w_ref, nf_ref)
    x1_o[...] = x1
    h2_o[...] = h2
    hi, lo = _split_bf16(h2)
    rhi = rhi_ref[...]
    logits = (jnp.dot(hi, rhi, preferred_element_type=F32) + jnp.dot(lo, rhi, preferred_element_type=F32)
              + jnp.dot(hi, rlo_ref[...], preferred_element_type=F32))
    lane = lax.broadcasted_iota(jnp.int32, logits.shape, 1)
    neg = -jnp.inf
    lg = jnp.where(lane < N_EXPERTS, logits, neg)
    m1 = jnp.max(lg, axis=-1, keepdims=True)
    i1 = jnp.min(jnp.where(lg == m1, lane, LANES), axis=-1, keepdims=True)
    lg2 = jnp.where(lane == i1, neg, lg)
    m2 = jnp.max(lg2, axis=-1, keepdims=True)
    i2 = jnp.min(jnp.where(lg2 == m2, lane, LANES), axis=-1, keepdims=True)
    e = jnp.exp(m2 - m1)
    g1 = 1.0 / (1.0 + e)
    g2 = e / (1.0 + e)
    route_o[...] = jnp.where(lane == 0, i1.astype(F32),
                             jnp.where(lane == 1, i2.astype(F32),
                                       jnp.where(lane == 2, g1, jnp.where(lane == 3, g2, 0.0))))


def _mix_out(o_sb, o_ca, o_ssm, x, g, w, nf, router=None):
    s = x.shape[0]
    row = lambda w_: pl.BlockSpec((TM, w_), lambda i: (i, 0))
    full = lambda a: pl.BlockSpec(a.shape, lambda i: (0, 0))
    ins = [o_sb, o_ca, o_ssm, x, g, w, nf]
    in_specs = [row(SB_WIDTH), row(CA_WIDTH), row(SSM_WIDTH), row(D_MODEL), full(g), full(w), full(nf)]
    if router is None:
        return pl.pallas_call(
            _mix_dense_kernel, grid=(s // TM,), in_specs=in_specs,
            out_specs=[row(D_MODEL)] * 2,
            out_shape=[jax.ShapeDtypeStruct((s, D_MODEL), F32), jax.ShapeDtypeStruct((s, D_MODEL), BF16)],
            compiler_params=_cparams(("parallel",)), name="mix_out_dense",
        )(*ins)
    return pl.pallas_call(
        _mix_moe_kernel, grid=(s // TM,), in_specs=in_specs + [full(router[0]), full(router[1])],
        out_specs=[row(D_MODEL), row(D_MODEL), row(LANES)],
        out_shape=[jax.ShapeDtypeStruct((s, D_MODEL), F32)] * 2 + [jax.ShapeDtypeStruct((s, LANES), F32)],
        compiler_params=_cparams(("parallel",)), name="mix_out_moe",
    )(*ins, *router)


def _swiglu_tile(xb, wg, wu, wd):
    hg = jnp.dot(xb, wg, preferred_element_type=F32)
    hu = jnp.dot(xb, wu, preferred_element_type=F32)
    hid = hg / (1.0 + jnp.exp(-hg)) * hu
    return jnp.dot(hid.astype(BF16), wd, preferred_element_type=F32)


def _ffn_kernel(h_ref, x_ref, wg_ref, wu_ref, wd_ref, o_ref, acc_ref):
    j = pl.program_id(1)
    y = _swiglu_tile(h_ref[...], wg_ref[...], wu_ref[...], wd_ref[...])

    @pl.when(j == 0)
    def _():
        acc_ref[...] = y

    @pl.when(j > 0)
    def _():
        acc_ref[...] += y

    @pl.when(j == pl.num_programs(1) - 1)
    def _():
        o_ref[...] = x_ref[...] + acc_ref[...]


def _ffn_dense(h2, x1, wg, wu, wd):
    s = x1.shape[0]
    row = pl.BlockSpec((TM, D_MODEL), lambda i, j: (i, 0))
    return pl.pallas_call(
        _ffn_kernel,
        grid=(s // TM, D_FF // FF_TILE),
        in_specs=[row, row,
                  pl.BlockSpec((D_MODEL, FF_TILE), lambda i, j: (0, j)),
                  pl.BlockSpec((D_MODEL, FF_TILE), lambda i, j: (0, j)),
                  pl.BlockSpec((FF_TILE, D_MODEL), lambda i, j: (j, 0))],
        out_specs=row,
        out_shape=jax.ShapeDtypeStruct((s, D_MODEL), F32),
        scratch_shapes=[pltpu.VMEM((TM, D_MODEL), F32)],
        compiler_params=_cparams(("parallel", "arbitrary")),
        name="ffn_dense",
    )(h2, x1, wg, wu, wd)


def _moe_kernel(blk_e, n_used, slot_tok, slot_dst, h_hbm, gate_ref, wg_ref, wu_ref, wd_ref, y_hbm,
                xbuf, acc_ref, ybuf, gsem, ssem):
    b = pl.program_id(0)
    j = pl.program_id(1)

    def gather(r):
        return pltpu.make_async_copy(h_hbm.at[pl.ds(slot_tok[0, 0, r], 1)], xbuf.at[pl.ds(r, 1)], gsem)

    def scatter(r):
        return pltpu.make_async_copy(ybuf.at[pl.ds(r, 1)], y_hbm.at[pl.ds(slot_dst[0, 0, r], 1)], ssem)

    def for_rows(fn):
        def body(r, c):
            fn(r)
            return c
        lax.fori_loop(0, MOE_BLOCK, body, 0)

    def for_filled_rows(fn):
        def body(r, c):
            @pl.when(slot_dst[0, 0, r] >= 0)
            def _():
                fn(r)
            return c
        lax.fori_loop(0, MOE_BLOCK, body, 0)

    @pl.when(b < n_used[0])
    def _():
        @pl.when(j == 0)
        def _():
            for_rows(lambda r: gather(r).start())
            for_rows(lambda r: gather(r).wait())

        y = _swiglu_tile(xbuf[...].astype(BF16), wg_ref[0], wu_ref[0], wd_ref[0])

        @pl.when(j == 0)
        def _():
            acc_ref[...] = y

        @pl.when(j > 0)
        def _():
            acc_ref[...] += y

        @pl.when(j == pl.num_programs(1) - 1)
        def _():
            ybuf[...] = acc_ref[...] * gate_ref[...]
            for_filled_rows(lambda r: scatter(r).start())
            for_filled_rows(lambda r: scatter(r).wait())


def _moe_experts(h2, route, wg, wu, wd):
    s = h2.shape[0]
    n_pairs = s * TOP_K
    n_blocks = n_pairs // MOE_BLOCK + N_EXPERTS
    n_slots = n_blocks * MOE_BLOCK
    nf = D_FF // FF_TILE
    flat_e = route[:, :TOP_K].astype(jnp.int32).reshape(-1)
    flat_g = route[:, TOP_K:2 * TOP_K].reshape(-1)
    onehot = (flat_e[:, None] == jnp.arange(N_EXPERTS)[None, :]).astype(jnp.int32)
    csum = jnp.cumsum(onehot, axis=0)
    counts = csum[-1]
    rank = jnp.take_along_axis(csum, flat_e[:, None], axis=1)[:, 0] - 1
    padded = (counts + MOE_BLOCK - 1) // MOE_BLOCK * MOE_BLOCK
    pend = jnp.cumsum(padded)
    pos = (pend - padded)[flat_e] + rank
    pair = jnp.arange(n_pairs, dtype=jnp.int32)
    slot_tok = jnp.zeros((n_slots,), jnp.int32).at[pos].set(pair // TOP_K)
    slot_dst = jnp.full((n_slots,), -1, jnp.int32).at[pos].set((pair % TOP_K) * s + pair // TOP_K)
    slot_gate = jnp.zeros((n_slots,), F32).at[pos].set(flat_g).reshape(n_slots, 1)
    blk_e = jnp.minimum(jnp.searchsorted(pend, jnp.arange(n_blocks) * MOE_BLOCK, side='right'),
                        N_EXPERTS - 1).astype(jnp.int32)
    n_used = (pend[-1] // MOE_BLOCK).astype(jnp.int32).reshape(1)

    def ff(b, j, used):
        return jnp.where(b < used[0], j, nf - 1)

    slots = pl.BlockSpec((1, 1, MOE_BLOCK), lambda b, j, e, u: (b, 0, 0), memory_space=pltpu.SMEM)
    grid_spec = pltpu.PrefetchScalarGridSpec(
        num_scalar_prefetch=2,
        grid=(n_blocks, nf),
        in_specs=[
            slots, slots,
            pl.BlockSpec(memory_space=pl.ANY),
            pl.BlockSpec((MOE_BLOCK, 1), lambda b, j, e, u: (b, 0)),
            pl.BlockSpec((1, D_MODEL, FF_TILE), lambda b, j, e, u: (e[b], 0, ff(b, j, u))),
            pl.BlockSpec((1, D_MODEL, FF_TILE), lambda b, j, e, u: (e[b], 0, ff(b, j, u))),
            pl.BlockSpec((1, FF_TILE, D_MODEL), lambda b, j, e, u: (e[b], ff(b, j, u), 0)),
        ],
        out_specs=pl.BlockSpec(memory_space=pl.ANY),
        scratch_shapes=[pltpu.VMEM((MOE_BLOCK, D_MODEL), F32)] * 3
        + [pltpu.SemaphoreType.DMA, pltpu.SemaphoreType.DMA],
    )
    return pl.pallas_call(
        _moe_kernel,
        grid_spec=grid_spec,
        out_shape=jax.ShapeDtypeStruct((n_pairs, D_MODEL), F32),
        compiler_params=_cparams(("arbitrary", "arbitrary")),
        name="moe_experts",
    )(blk_e, n_used, slot_tok.reshape(n_blocks, 1, MOE_BLOCK), slot_dst.reshape(n_blocks, 1, MOE_BLOCK),
      h2, slot_gate, wg, wu, wd)


def _combine_kernel(x_ref, y0_ref, y1_ref, o_ref):
    o_ref[...] = x_ref[...] + (y0_ref[...] + y1_ref[...])


def _moe_combine(x1, y):
    s = x1.shape[0]
    nt = s // TM
    return pl.pallas_call(
        _combine_kernel,
        grid=(nt,),
        in_specs=[pl.BlockSpec((TM, D_MODEL), lambda i: (i, 0)),
                  pl.BlockSpec((TM, D_MODEL), lambda i: (i, 0)),
                  pl.BlockSpec((TM, D_MODEL), lambda i: (i + nt, 0))],
        out_specs=pl.BlockSpec((TM, D_MODEL), lambda i: (i, 0)),
        out_shape=jax.ShapeDtypeStruct((s, D_MODEL), F32),
        compiler_params=_cparams(("parallel",)),
        name="moe_combine",
    )(x1, y, y)


def _block_diag_ones(width, group):
    idx = jnp.arange(width) // group
    return (idx[:, None] == idx[None, :]).astype(BF16)


def kernel(x, norm_mix, w_in, q_norm_sb, k_norm_sb, q_norm_ca, k_norm_ca, rel_bias, lam_re, lam_im, log_dt, b_re, b_im, c_re, c_im, d_skip, w_glu, g_out, w_out, norm_ffn, w_gate_dense, w_up_dense, w_down_dense, w_router, w_gate_moe, w_up_moe, w_down_moe):
    b, s, d = x.shape
    assert b == 1 and d == D_MODEL and s % (TM * 2) == 0
    xs = x.reshape(s, d).astype(F32)
    scale = 1.0 / math.sqrt(HEAD_DIM)
    gm = _block_diag_ones(CA_WIDTH, HEAD_DIM)
    idx = jnp.arange(SB_BLOCK)
    tri = (idx[:, None] > idx[None, :]).astype(BF16)
    vec = lambda a: a.reshape(1, -1).astype(F32)
    tile = lambda a, width: jnp.tile(a.astype(F32), width // HEAD_DIM).reshape(1, width)

    for l in range(DEPTH):
        q_sb, k_sb, v_sb, q_ca, k_ca, v_ca, u = _in_proj(
            xs, vec(norm_mix[l]), w_in[l].astype(BF16),
            tile(q_norm_sb[l], SB_WIDTH) * scale, tile(k_norm_sb[l], SB_WIDTH),
            tile(q_norm_ca[l], CA_WIDTH) * scale, tile(k_norm_ca[l], CA_WIDTH), gm)
        o_sb = _sb_attention(q_sb, k_sb, v_sb, tri)
        o_ca = _ca_attention(q_ca, k_ca, v_ca, _ca_bias_mask(rel_bias[l]))
        o_ssm = _s5_mixer(u, lam_re[l], lam_im[l], log_dt[l], b_re[l], b_im[l], c_re[l], c_im[l],
                          d_skip[l], w_glu[l])
        i = l // 2
        if l % 2 == 0:
            x1, h2 = _mix_out(o_sb, o_ca, o_ssm, xs, vec(g_out[l]), w_out[l].astype(BF16), vec(norm_ffn[l]))
            xs = _ffn_dense(h2, x1, w_gate_dense[i].astype(BF16), w_up_dense[i].astype(BF16),
                            w_down_dense[i].astype(BF16))
        else:
            wr = jnp.pad(w_router[i].astype(F32), ((0, 0), (0, LANES - N_EXPERTS)))
            r_hi, r_lo = _split_bf16(wr)
            x1, h2, route = _mix_out(o_sb, o_ca, o_ssm, xs, vec(g_out[l]), w_out[l].astype(BF16),
                                     vec(norm_ffn[l]), router=(r_hi, r_lo))
            y = _moe_experts(h2, route, w_gate_moe[i].astype(BF16), w_up_moe[i].astype(BF16),
                             w_down_moe[i].astype(BF16))
            xs = _moe_combine(x1, y)
    return xs.reshape(b, s, d).astype(x.dtype)
```

```python
import functools
import math

import jax
import jax.numpy as jnp
import numpy as np
from jax import lax
from jax.experimental import pallas as pl
from jax.experimental.pallas import tpu as pltpu

F32 = jnp.float32
BF16 = jnp.bfloat16

D_MODEL = 1024
DEPTH = 4
CHUNK = 64
HEAD_DIM = 64
SB_WIDTH = 256
CA_WIDTH = 512
CA_HEADS = 8
SSM_WIDTH = 256
SSM_GROUP_CH = 16
SSM_GROUPS = 16
SSM_STATE = 64
N_STATES = SSM_GROUPS * SSM_STATE
D_IN_PROJ = 3 * SB_WIDTH + 3 * CA_WIDTH + SSM_WIDTH
MASK_LOGIT = -1e4
LOOKBACK = 8
REL_CLIP = 128
D_FF = 2816
N_EXPERTS = 8
TOP_K = 2
RMS_EPS = 1e-6

LANES = 128
SUBLANES = 8
VMEM_LIMIT = 56 * 1024 * 1024

TM = 512
SB_BLOCK = 256
CA_BLOCK = 256
FF_TILE = 1408
MOE_BLOCK = 512
SB_UNDERFLOW = 104.0


def _cparams(sem):
    return pltpu.CompilerParams(dimension_semantics=sem, vmem_limit_bytes=VMEM_LIMIT)


def _split_bf16(x):
    hi = x.astype(BF16)
    lo = (x - hi.astype(F32)).astype(BF16)
    return hi, lo


def _rms(x, gain):
    ms = jnp.mean(x * x, axis=-1, keepdims=True)
    return x * lax.rsqrt(ms + RMS_EPS) * gain


def _head_rms(p, gmat, gain):
    hi, lo = _split_bf16(p * p)
    ss = jnp.dot(hi, gmat, preferred_element_type=F32) + jnp.dot(lo, gmat, preferred_element_type=F32)
    return p * lax.rsqrt(ss * (1.0 / HEAD_DIM) + RMS_EPS) * gain


def _in_proj_kernel(x_ref, g_ref, w_ref, gq_sb, gk_sb, gq_ca, gk_ca, gm_ref,
                    qsb_o, ksb_o, vsb_o, qca_o, kca_o, vca_o, u_o):
    h = _rms(x_ref[...], g_ref[...])
    proj = jnp.dot(h.astype(BF16), w_ref[...], preferred_element_type=F32)
    gm_sb = gm_ref[0:SB_WIDTH, 0:SB_WIDTH]
    gm_ca = gm_ref[...]
    o1, o2, o3 = SB_WIDTH, 2 * SB_WIDTH, 3 * SB_WIDTH
    o4, o5, o6 = o3 + CA_WIDTH, o3 + 2 * CA_WIDTH, o3 + 3 * CA_WIDTH
    qsb_o[...] = _head_rms(proj[:, :o1], gm_sb, gq_sb[...]).astype(BF16)
    ksb_o[...] = _head_rms(proj[:, o1:o2], gm_sb, gk_sb[...]).astype(BF16)
    vsb_o[...] = proj[:, o2:o3].astype(BF16)
    qca_o[...] = _head_rms(proj[:, o3:o4], gm_ca, gq_ca[...]).astype(BF16)
    kca_o[...] = _head_rms(proj[:, o4:o5], gm_ca, gk_ca[...]).astype(BF16)
    vca_o[...] = proj[:, o5:o6].astype(BF16)
    u_o[...] = proj[:, o6:]


def _in_proj(x, g, w, gq_sb, gk_sb, gq_ca, gk_ca, gm):
    s = x.shape[0]
    row = lambda w_: pl.BlockSpec((TM, w_), lambda i: (i, 0))
    full = lambda a: pl.BlockSpec(a.shape, lambda i: (0,) * a.ndim)
    return pl.pallas_call(
        _in_proj_kernel,
        grid=(s // TM,),
        in_specs=[row(D_MODEL)] + [full(a) for a in (g, w, gq_sb, gk_sb, gq_ca, gk_ca, gm)],
        out_specs=[row(SB_WIDTH)] * 3 + [row(CA_WIDTH)] * 3 + [row(SSM_WIDTH)],
        out_shape=[jax.ShapeDtypeStruct((s, SB_WIDTH), BF16)] * 3
        + [jax.ShapeDtypeStruct((s, CA_WIDTH), BF16)] * 3
        + [jax.ShapeDtypeStruct((s, SSM_WIDTH), F32)],
        compiler_params=_cparams(("parallel",)),
        name="in_proj",
    )(x, g, w, gq_sb, gk_sb, gq_ca, gk_ca, gm)


def _sb_kernel(q_ref, k_ref, v_ref, tri_ref, o_ref):
    i = pl.program_id(0)
    tri = tri_ref[...]
    row = lax.broadcasted_iota(jnp.int32, (SB_BLOCK, SB_BLOCK), 0)
    col = lax.broadcasted_iota(jnp.int32, (SB_BLOCK, SB_BLOCK), 1)
    rel = col - row
    lane = lax.broadcasted_iota(jnp.int32, (SB_BLOCK, LANES), 1)
    n_pairs = SB_WIDTH // LANES
    per_pair = LANES // HEAD_DIM
    in_head = [(lane >= hh * HEAD_DIM) & (lane < (hh + 1) * HEAD_DIM) for hh in range(per_pair)]
    qm = [jnp.where(in_head[hh], q_ref[:, p * LANES:(p + 1) * LANES].astype(F32), 0.0).astype(BF16)
          for p in range(n_pairs) for hh in range(per_pair)]

    def cond(c):
        j, done = c[0], c[1]
        return jnp.logical_and(j >= 0, jnp.logical_not(done))

    def body(c):
        j, _, carries, accs = c
        start = pl.multiple_of(j * SB_BLOCK, SB_BLOCK)
        causal = rel < (i - j) * SB_BLOCK
        new_carries, new_accs = [], []
        for p in range(n_pairs):
            lanes = slice(p * LANES, (p + 1) * LANES)
            kb = k_ref[pl.ds(start, SB_BLOCK), lanes]
            vb = v_ref[pl.ds(start, SB_BLOCK), lanes]
            acc = accs[p]
            for hh in range(per_pair):
                carry = carries[p * per_pair + hh]
                z = lax.dot_general(qm[p * per_pair + hh], kb, (((1,), (1,)), ((), ())),
                                    preferred_element_type=F32)
                z = jnp.where(causal, z, MASK_LOGIT)
                sp = jnp.maximum(z, 0.0) + jnp.log(1.0 + jnp.exp(-jnp.abs(z)))
                hi, lo = _split_bf16(sp)
                later = (jnp.dot(hi, tri, preferred_element_type=F32)
                         + jnp.dot(lo, tri, preferred_element_type=F32))
                w = jnp.exp(z - sp - (later + carry))
                pv = jnp.dot(w.astype(BF16), vb, preferred_element_type=F32)
                acc = acc + jnp.where(in_head[hh], pv, 0.0)
                new_carries.append(carry + jnp.sum(sp, axis=-1, keepdims=True))
            new_accs.append(acc)
        done = jnp.min(functools.reduce(jnp.minimum, new_carries)) > SB_UNDERFLOW
        return j - 1, done, tuple(new_carries), tuple(new_accs)

    init = (i, False,
            tuple(jnp.zeros((SB_BLOCK, 1), F32) for _ in range(n_pairs * per_pair)),
            tuple(jnp.zeros((SB_BLOCK, LANES), F32) for _ in range(n_pairs)))
    accs = lax.while_loop(cond, body, init)[3]
    for p in range(n_pairs):
        o_ref[:, p * LANES:(p + 1) * LANES] = accs[p]


def _sb_attention(q, k, v, tri):
    s = q.shape[0]
    whole = pl.BlockSpec(memory_space=pltpu.VMEM)
    return pl.pallas_call(
        _sb_kernel,
        grid=(s // SB_BLOCK,),
        in_specs=[pl.BlockSpec((SB_BLOCK, SB_WIDTH), lambda i: (i, 0)), whole, whole, whole],
        out_specs=pl.BlockSpec((SB_BLOCK, SB_WIDTH), lambda i: (i, 0)),
        out_shape=jax.ShapeDtypeStruct((s, SB_WIDTH), F32),
        compiler_params=_cparams(("parallel",)),
        name="sb_attention",
    )(q, k, v, tri)


CA_WINDOW = 3


def _ca_kernel(q_ref, k0, k1, k2, v0, v1, v2, bm_ref, o_ref):
    i = pl.program_id(0)
    k_refs = (k0, k1, k2)
    v_refs = (v0, v1, v2)
    lane = lax.broadcasted_iota(jnp.int32, (CA_BLOCK, LANES), 1)
    for pair in range(CA_WIDTH // LANES):
        lanes = slice(pair * LANES, (pair + 1) * LANES)
        q_pair = q_ref[:, lanes].astype(F32)
        out_pair = jnp.zeros((CA_BLOCK, LANES), F32)
        for hh in range(LANES // HEAD_DIM):
            h = pair * (LANES // HEAD_DIM) + hh
            in_head = (lane >= hh * HEAD_DIM) & (lane < (hh + 1) * HEAD_DIM)
            qm = jnp.where(in_head, q_pair, 0.0).astype(BF16)
            scores = []
            for jj in range(CA_WINDOW):
                sc = lax.dot_general(qm, k_refs[jj][:, lanes], (((1,), (1,)), ((), ())),
                                     preferred_element_type=F32)
                sc = sc + bm_ref[h, :, jj * CA_BLOCK:(jj + 1) * CA_BLOCK]
                scores.append(jnp.where(i + jj >= CA_WINDOW - 1, sc, -1e30))
            m = functools.reduce(jnp.maximum, [jnp.max(sc, axis=-1, keepdims=True) for sc in scores])
            denom = jnp.zeros((CA_BLOCK, 1), F32)
            pv = jnp.zeros((CA_BLOCK, LANES), F32)
            for jj in range(CA_WINDOW):
                p = jnp.exp(scores[jj] - m)
                denom = denom + jnp.sum(p, axis=-1, keepdims=True)
                pv = pv + jnp.dot(p.astype(BF16), v_refs[jj][:, lanes], preferred_element_type=F32)
            out_pair = out_pair + jnp.where(in_head, pv / denom, 0.0)
        o_ref[:, lanes] = out_pair


def _ca_attention(q, k, v, bm):
    s = q.shape[0]
    blk = lambda off: pl.BlockSpec((CA_BLOCK, CA_WIDTH), lambda i: (jnp.maximum(i - off, 0), 0))
    return pl.pallas_call(
        _ca_kernel,
        grid=(s // CA_BLOCK,),
        in_specs=[blk(0), blk(2), blk(1), blk(0), blk(2), blk(1), blk(0),
                  pl.BlockSpec(bm.shape, lambda i: (0, 0, 0))],
        out_specs=blk(0),
        out_shape=jax.ShapeDtypeStruct((s, CA_WIDTH), F32),
        compiler_params=_cparams(("parallel",)),
        name="ca_attention",
    )(q, k, k, k, v, v, v, bm)


def _ca_bias_mask(rel_bias):
    width = CA_WINDOW * CA_BLOCK
    period = CA_BLOCK + width - 1
    m = (np.arange(period) + CA_BLOCK - 1) % period
    dist = LOOKBACK * CHUNK + (CA_BLOCK - 1) - m
    by_offset = rel_bias.astype(F32)[:, np.clip(dist, -REL_CLIP, REL_CLIP) + REL_CLIP]
    toeplitz = jnp.tile(by_offset, (1, CA_BLOCK))[:, :CA_BLOCK * (period - 1)]
    toeplitz = toeplitz.reshape(-1, CA_BLOCK, period - 1)[:, :, :width]
    r = np.arange(CA_BLOCK)[:, None]
    c = np.arange(width)[None, :]
    back = r // CHUNK - (c // CHUNK - LOOKBACK)
    valid = (back >= 0) & (back <= LOOKBACK)
    return jnp.where(jnp.asarray(valid)[None], toeplitz, -1e30)


def _s5_disc_kernel(lre_ref, lim_ref, ldt_ref, bre_ref, bim_ref, are_o, aim_o, bbre_o, bbim_o):
    lre = lre_ref[...]
    lim = lim_ref[...]
    dt = jnp.exp(ldt_ref[...])
    mag = jnp.exp(lre * dt)
    ang = lim * dt
    a_re = mag * jnp.cos(ang)
    a_im = mag * jnp.sin(ang)
    are_o[...] = a_re
    aim_o[...] = a_im
    nre = a_re - 1.0
    nim = a_im
    den = lre * lre + lim * lim
    zoh_re = (nre * lre + nim * lim) / den
    zoh_im = (nim * lre - nre * lim) / den
    bre = bre_ref[...]
    bim = bim_ref[...]
    bbre_o[...] = zoh_re * bre - zoh_im * bim
    bbim_o[...] = zoh_re * bim + zoh_im * bre


def _s5_discretize(lam_re, lam_im, log_dt, b_re, b_im):
    col = lambda a: a.reshape(N_STATES, 1).astype(F32)
    ldt = jnp.repeat(log_dt.astype(F32), SSM_STATE).reshape(N_STATES, 1)
    b2 = lambda a: a.reshape(N_STATES, SSM_GROUP_CH).astype(F32)
    return pl.pallas_call(
        _s5_disc_kernel,
        out_shape=[jax.ShapeDtypeStruct((N_STATES, 1), F32)] * 2
        + [jax.ShapeDtypeStruct((N_STATES, SSM_GROUP_CH), F32)] * 2,
        name="s5_discretize",
    )(col(lam_re), col(lam_im), ldt, b2(b_re), b2(b_im))


S5_SEGS = SUBLANES
S5_STEPS = 64


def _cmul(a_re, a_im, b_re, b_im):
    return a_re * b_re - a_im * b_im, a_re * b_im + a_im * b_re


def _s5_local_kernel(u_ref, wre_ref, wim_ref, are_ref, aim_ref, cre_ref, cim_ref, y_o, end_o,
                     xre, xim, st):
    @pl.when(pl.program_id(0) == 0)
    def _():
        st[...] = jnp.zeros_like(st)

    u = u_ref[...].astype(BF16)
    xre[...] = jnp.dot(u, wre_ref[...], preferred_element_type=F32)
    xim[...] = jnp.dot(u, wim_ref[...], preferred_element_type=F32)
    a_re = jnp.broadcast_to(are_ref[...], (S5_SEGS, N_STATES))
    a_im = jnp.broadcast_to(aim_ref[...], (S5_SEGS, N_STATES))

    def step(k, c):
        rows = pl.ds(pl.multiple_of(k * S5_SEGS, S5_SEGS), S5_SEGS)
        p_re, p_im = _cmul(a_re, a_im, *c)
        n_re = p_re + xre[rows, :]
        n_im = p_im + xim[rows, :]
        xre[rows, :] = n_re
        xim[rows, :] = n_im
        return n_re, n_im

    x_re, x_im = lax.fori_loop(0, S5_STEPS, step, (st[0], st[1]), unroll=4)
    st[0] = x_re
    st[1] = x_im
    end_o[0] = x_re
    end_o[1] = x_im
    y_o[...] = (jnp.dot(xre[...].astype(BF16), cre_ref[...], preferred_element_type=F32)
                - jnp.dot(xim[...].astype(BF16), cim_ref[...], preferred_element_type=F32))


def _s5_finish_kernel(seg_len, y_ref, u_ref, end_ref, are_ref, aim_ref, cre_ref, cim_ref, d_ref, wglu_ref,
                      o_ref, zre, zim, x0, pw):
    a_re = jnp.broadcast_to(are_ref[...], (S5_SEGS, N_STATES))
    a_im = jnp.broadcast_to(aim_ref[...], (S5_SEGS, N_STATES))

    @pl.when(pl.program_id(0) == 0)
    def _():
        r_re, r_im = are_ref[...], aim_ref[...]
        p_re, p_im = jnp.ones_like(r_re), jnp.zeros_like(r_re)
        n = seg_len
        while n:
            if n & 1:
                p_re, p_im = _cmul(p_re, p_im, r_re, r_im)
            r_re, r_im = _cmul(r_re, r_im, r_re, r_im)
            n >>= 1
        rows_re, rows_im = [jnp.zeros_like(p_re)], [jnp.zeros_like(p_re)]
        for s in range(1, S5_SEGS):
            c_re, c_im = _cmul(p_re, p_im, rows_re[-1], rows_im[-1])
            rows_re.append(c_re + end_ref[0, s - 1:s, :])
            rows_im.append(c_im + end_ref[1, s - 1:s, :])
        x0[0] = jnp.concatenate(rows_re, axis=0)
        x0[1] = jnp.concatenate(rows_im, axis=0)
        pw[0] = a_re
        pw[1] = a_im

    x0_re = x0[0]
    x0_im = x0[1]

    def step(k, c):
        rows = pl.ds(pl.multiple_of(k * S5_SEGS, S5_SEGS), S5_SEGS)
        z_re, z_im = _cmul(c[0], c[1], x0_re, x0_im)
        zre[rows, :] = z_re
        zim[rows, :] = z_im
        return _cmul(a_re, a_im, *c)

    p_re, p_im = lax.fori_loop(0, S5_STEPS, step, (pw[0], pw[1]), unroll=4)
    pw[0] = p_re
    pw[1] = p_im
    y = y_ref[...] + (jnp.dot(zre[...].astype(BF16), cre_ref[...], preferred_element_type=F32)
                      - jnp.dot(zim[...].astype(BF16), cim_ref[...], preferred_element_type=F32))
    y = y + d_ref[...] * u_ref[...]
    y = 0.5 * y * (1.0 + jnp.tanh(math.sqrt(2.0 / math.pi) * (y + 0.044715 * (y * y * y))))
    hg = jnp.dot(y.astype(BF16), wglu_ref[...], preferred_element_type=F32)
    o_ref[...] = hg[:, :SSM_WIDTH] / (1.0 + jnp.exp(-hg[:, SSM_WIDTH:]))


def _s5_mixer(u, lam_re, lam_im, log_dt, b_re, b_im, c_re, c_im, d_skip, w_glu):
    s = u.shape[0]
    seg_len = s // S5_SEGS
    rows = S5_STEPS * S5_SEGS
    a_re, a_im, bb_re, bb_im = _s5_discretize(lam_re, lam_im, log_dt, b_re, b_im)
    a_re = a_re.reshape(1, N_STATES)
    a_im = a_im.reshape(1, N_STATES)
    eye = jnp.eye(SSM_GROUPS, dtype=F32)

    def in_matrix(bb):
        bb = bb.reshape(SSM_GROUPS, SSM_STATE, SSM_GROUP_CH)
        return jnp.einsum('gpc,gh->gchp', bb, eye).reshape(SSM_WIDTH, N_STATES).astype(BF16)

    def out_matrix(c):
        return jnp.einsum('gcp,gh->gphc', c.astype(F32), eye).reshape(N_STATES, SSM_WIDTH).astype(BF16)

    w_re, w_im = in_matrix(bb_re), in_matrix(bb_im)
    cm_re, cm_im = out_matrix(c_re), out_matrix(c_im)
    d = d_skip.reshape(1, SSM_WIDTH).astype(F32)
    wg = w_glu.astype(BF16)
    u_seg = u.reshape(S5_SEGS, seg_len, SSM_WIDTH).transpose(1, 0, 2).reshape(s, SSM_WIDTH)
    full = lambda a: pl.BlockSpec(a.shape, lambda i: (0,) * a.ndim)
    row = pl.BlockSpec((rows, SSM_WIDTH), lambda i: (i, 0))
    ends = pl.BlockSpec((2, S5_SEGS, N_STATES), lambda i: (0, 0, 0))
    big = pltpu.VMEM((rows, N_STATES), F32)
    small = pltpu.VMEM((2, S5_SEGS, N_STATES), F32)
    y_local, end = pl.pallas_call(
        _s5_local_kernel,
        grid=(seg_len // S5_STEPS,),
        in_specs=[row] + [full(a) for a in (w_re, w_im, a_re, a_im, cm_re, cm_im)],
        out_specs=[row, ends],
        out_shape=[jax.ShapeDtypeStruct((s, SSM_WIDTH), F32),
                   jax.ShapeDtypeStruct((2, S5_SEGS, N_STATES), F32)],
        scratch_shapes=[big, big, small],
        compiler_params=_cparams(("arbitrary",)),
        name="s5_local",
    )(u_seg, w_re, w_im, a_re, a_im, cm_re, cm_im)
    o_seg = pl.pallas_call(
        functools.partial(_s5_finish_kernel, seg_len),
        grid=(seg_len // S5_STEPS,),
        in_specs=[row, row, ends] + [full(a) for a in (a_re, a_im, cm_re, cm_im, d, wg)],
        out_specs=row,
        out_shape=jax.ShapeDtypeStruct((s, SSM_WIDTH), F32),
        scratch_shapes=[big, big, small, small],
        compiler_params=_cparams(("arbitrary",)),
        name="s5_finish",
    )(y_local, u_seg, end, a_re, a_im, cm_re, cm_im, d, wg)
    return o_seg.reshape(seg_len, S5_SEGS, SSM_WIDTH).transpose(1, 0, 2).reshape(s, SSM_WIDTH)


def _mix_core(osb_ref, oca_ref, ossm_ref, x_ref, g_ref, w_ref, nf_ref):
    g = g_ref[...]
    mixed = jnp.concatenate([
        _rms(osb_ref[...], g[:, :SB_WIDTH]),
        _rms(oca_ref[...], g[:, SB_WIDTH:SB_WIDTH + CA_WIDTH]),
        _rms(ossm_ref[...], g[:, SB_WIDTH + CA_WIDTH:]),
    ], axis=-1)
    x1 = x_ref[...] + jnp.dot(mixed.astype(BF16), w_ref[...], preferred_element_type=F32)
    return x1, _rms(x1, nf_ref[...])


def _mix_dense_kernel(osb_ref, oca_ref, ossm_ref, x_ref, g_ref, w_ref, nf_ref, x1_o, h2_o):
    x1, h2 = _mix_core(osb_ref, oca_ref, ossm_ref, x_ref, g_ref, w_ref, nf_ref)
    x1_o[...] = x1
    h2_o[...] = h2.astype(BF16)


def _mix_moe_kernel(osb_ref, oca_ref, ossm_ref, x_ref, g_ref, w_ref, nf_ref, rhi_ref, rlo_ref,
                    x1_o, h2_o, route_o):
    x1, h2 = _mix_core(osb_ref, oca_ref, ossm_ref, x_ref, g_ref, w_ref, nf_ref)
    x1_o[...] = x1
    h2_o[...] = h2
    hi, lo = _split_bf16(h2)
    rhi = rhi_ref[...]
    logits = (jnp.dot(hi, rhi, preferred_element_type=F32) + jnp.dot(lo, rhi, preferred_element_type=F32)
              + jnp.dot(hi, rlo_ref[...], preferred_element_type=F32))
    lane = lax.broadcasted_iota(jnp.int32, logits.shape, 1)
    neg = -jnp.inf
    lg = jnp.where(lane < N_EXPERTS, logits, neg)
    m1 = jnp.max(lg, axis=-1, keepdims=True)
    i1 = jnp.min(jnp.where(lg == m1, lane, LANES), axis=-1, keepdims=True)
    lg2 = jnp.where(lane == i1, neg, lg)
    m2 = jnp.max(lg2, axis=-1, keepdims=True)
    i2 = jnp.min(jnp.where(lg2 == m2, lane, LANES), axis=-1, keepdims=True)
    e = jnp.exp(m2 - m1)
    g1 = 1.0 / (1.0 + e)
    g2 = e / (1.0 + e)
    route_o[...] = jnp.where(lane == 0, i1.astype(F32),
                             jnp.where(lane == 1, i2.astype(F32),
                                       jnp.where(lane == 2, g1, jnp.where(lane == 3, g2, 0.0))))


def _mix_out(o_sb, o_ca, o_ssm, x, g, w, nf, router=None):
    s = x.shape[0]
    row = lambda w_: pl.BlockSpec((TM, w_), lambda i: (i, 0))
    full = lambda a: pl.BlockSpec(a.shape, lambda i: (0, 0))
    ins = [o_sb, o_ca, o_ssm, x, g, w, nf]
    in_specs = [row(SB_WIDTH), row(CA_WIDTH), row(SSM_WIDTH), row(D_MODEL), full(g), full(w), full(nf)]
    if router is None:
        return pl.pallas_call(
            _mix_dense_kernel, grid=(s // TM,), in_specs=in_specs,
            out_specs=[row(D_MODEL)] * 2,
            out_shape=[jax.ShapeDtypeStruct((s, D_MODEL), F32), jax.ShapeDtypeStruct((s, D_MODEL), BF16)],
            compiler_params=_cparams(("parallel",)), name="mix_out_dense",
        )(*ins)
    return pl.pallas_call(
        _mix_moe_kernel, grid=(s // TM,), in_specs=in_specs + [full(router[0]), full(router[1])],
        out_specs=[row(D_MODEL), row(D_MODEL), row(LANES)],
        out_shape=[jax.ShapeDtypeStruct((s, D_MODEL), F32)] * 2 + [jax.ShapeDtypeStruct((s, LANES), F32)],
        compiler_params=_cparams(("parallel",)), name="mix_out_moe",
    )(*ins, *router)


def _swiglu_tile(xb, wg, wu, wd):
    hg = jnp.dot(xb, wg, preferred_element_type=F32)
    hu = jnp.dot(xb, wu, preferred_element_type=F32)
    hid = hg / (1.0 + jnp.exp(-hg)) * hu
    return jnp.dot(hid.astype(BF16), wd, preferred_element_type=F32)


def _ffn_kernel(h_ref, x_ref, wg_ref, wu_ref, wd_ref, o_ref, acc_ref):
    j = pl.program_id(1)
    y = _swiglu_tile(h_ref[...], wg_ref[...], wu_ref[...], wd_ref[...])

    @pl.when(j == 0)
    def _():
        acc_ref[...] = y

    @pl.when(j > 0)
    def _():
        acc_ref[...] += y

    @pl.when(j == pl.num_programs(1) - 1)
    def _():
        o_ref[...] = x_ref[...] + acc_ref[...]


def _ffn_dense(h2, x1, wg, wu, wd):
    s = x1.shape[0]
    row = pl.BlockSpec((TM, D_MODEL), lambda i, j: (i, 0))
    return pl.pallas_call(
        _ffn_kernel,
        grid=(s // TM, D_FF // FF_TILE),
        in_specs=[row, row,
                  pl.BlockSpec((D_MODEL, FF_TILE), lambda i, j: (0, j)),
                  pl.BlockSpec((D_MODEL, FF_TILE), lambda i, j: (0, j)),
                  pl.BlockSpec((FF_TILE, D_MODEL), lambda i, j: (j, 0))],
        out_specs=row,
        out_shape=jax.ShapeDtypeStruct((s, D_MODEL), F32),
        scratch_shapes=[pltpu.VMEM((TM, D_MODEL), F32)],
        compiler_params=_cparams(("parallel", "arbitrary")),
        name="ffn_dense",
    )(h2, x1, wg, wu, wd)


ROW_UNROLL = 8


def _for_rows(n, fn):
    def body(r, c):
        fn(r)
        return c
    lax.fori_loop(0, n, body, 0, unroll=ROW_UNROLL)


def _dispatch_kernel(fill_blk, n_used, pos_ref, h_ref, xs_hbm, zbuf, sem, zsem):
    n_blocks = xs_hbm.shape[0] // MOE_BLOCK

    def zero_fill(blk):
        return pltpu.make_async_copy(
            zbuf, xs_hbm.at[pl.ds(pl.multiple_of(blk * MOE_BLOCK, MOE_BLOCK), MOE_BLOCK)], zsem)

    @pl.when(pl.program_id(0) == 0)
    def _():
        zbuf[...] = jnp.zeros_like(zbuf)
        def fill(blk):
            cp = zero_fill(blk)
            cp.start()
            cp.wait()

        for e in range(N_EXPERTS):
            fill(fill_blk[e])

        def tail(blk, c):
            fill(blk)
            return c
        lax.fori_loop(n_used[0], n_blocks, tail, 0)

    def row_copy(r, k):
        return pltpu.make_async_copy(h_ref.at[pl.ds(r, 1)], xs_hbm.at[pl.ds(pos_ref[0, 0, TOP_K * r + k], 1)], sem)

    def start(r):
        for k in range(TOP_K):
            row_copy(r, k).start()

    def wait(r):
        for k in range(TOP_K):
            row_copy(r, k).wait()

    _for_rows(TM, start)
    _for_rows(TM, wait)


def _moe_dispatch(h2, pos, fill_blk, n_used, n_blocks):
    s = h2.shape[0]
    grid_spec = pltpu.PrefetchScalarGridSpec(
        num_scalar_prefetch=2,
        grid=(s // TM,),
        in_specs=[pl.BlockSpec((1, 1, TOP_K * TM), lambda i, f, u: (i, 0, 0), memory_space=pltpu.SMEM),
                  pl.BlockSpec((TM, D_MODEL), lambda i, f, u: (i, 0))],
        out_specs=pl.BlockSpec(memory_space=pl.ANY),
        scratch_shapes=[pltpu.VMEM((MOE_BLOCK, D_MODEL), F32), pltpu.SemaphoreType.DMA,
                        pltpu.SemaphoreType.DMA],
    )
    return pl.pallas_call(
        _dispatch_kernel,
        grid_spec=grid_spec,
        out_shape=jax.ShapeDtypeStruct((n_blocks * MOE_BLOCK, D_MODEL), F32),
        compiler_params=_cparams(("arbitrary",)),
        name="moe_dispatch",
    )(fill_blk, n_used, pos.reshape(s // TM, 1, TOP_K * TM), h2)


def _moe_kernel(blk_e, n_used, xs_ref, wg_ref, wu_ref, wd_ref, o_ref, xb_ref, acc_ref):
    b = pl.program_id(0)
    j = pl.program_id(1)
    last = pl.num_programs(1) - 1

    @pl.when(b < n_used[0])
    def _():
        @pl.when(j == 0)
        def _():
            xb_ref[...] = xs_ref[...].astype(BF16)

        y = _swiglu_tile(xb_ref[...], wg_ref[0], wu_ref[0], wd_ref[0])

        @pl.when(j == 0)
        def _():
            acc_ref[...] = y

        @pl.when(j > 0)
        def _():
            acc_ref[...] += y

        @pl.when(j == last)
        def _():
            o_ref[...] = acc_ref[...]

    @pl.when(jnp.logical_and(b >= n_used[0], j == last))
    def _():
        o_ref[...] = jnp.zeros_like(o_ref)


def _moe_experts(xs, blk_e, n_used, wg, wu, wd, n_blocks):
    nf = D_FF // FF_TILE

    def ff(b, j, used):
        return jnp.where(b < used[0], j, nf - 1)

    grid_spec = pltpu.PrefetchScalarGridSpec(
        num_scalar_prefetch=2,
        grid=(n_blocks, nf),
        in_specs=[
            pl.BlockSpec((MOE_BLOCK, D_MODEL), lambda b, j, e, u: (jnp.minimum(b, u[0] - 1), 0)),
            pl.BlockSpec((1, D_MODEL, FF_TILE), lambda b, j, e, u: (e[b], 0, ff(b, j, u))),
            pl.BlockSpec((1, D_MODEL, FF_TILE), lambda b, j, e, u: (e[b], 0, ff(b, j, u))),
            pl.BlockSpec((1, FF_TILE, D_MODEL), lambda b, j, e, u: (e[b], ff(b, j, u), 0)),
        ],
        out_specs=pl.BlockSpec((MOE_BLOCK, D_MODEL), lambda b, j, e, u: (b, 0)),
        scratch_shapes=[pltpu.VMEM((MOE_BLOCK, D_MODEL), BF16), pltpu.VMEM((MOE_BLOCK, D_MODEL), F32)],
    )
    return pl.pallas_call(
        _moe_kernel,
        grid_spec=grid_spec,
        out_shape=jax.ShapeDtypeStruct((n_blocks * MOE_BLOCK, D_MODEL), F32),
        compiler_params=_cparams(("arbitrary", "arbitrary")),
        name="moe_experts",
    )(blk_e, n_used, xs, wg, wu, wd)


def _combine_kernel(pos_ref, x_ref, route_ref, ys_hbm, o_ref, ybuf, sem):
    def row_copy(r, k):
        return pltpu.make_async_copy(ys_hbm.at[pl.ds(pos_ref[0, 0, TOP_K * r + k], 1)],
                                     ybuf.at[k, pl.ds(r, 1)], sem)

    def start(r):
        for k in range(TOP_K):
            row_copy(r, k).start()

    def wait(r):
        for k in range(TOP_K):
            row_copy(r, k).wait()

    _for_rows(TM, start)
    _for_rows(TM, wait)
    route = route_ref[...]
    gated = [ybuf[k] * route[:, TOP_K + k:TOP_K + k + 1] for k in range(TOP_K)]
    o_ref[...] = x_ref[...] + functools.reduce(lambda a, b_: a + b_, gated)


def _moe_combine(x1, route, pos, ys):
    s = x1.shape[0]
    row = lambda w_: pl.BlockSpec((TM, w_), lambda i: (i, 0))
    return pl.pallas_call(
        _combine_kernel,
        grid=(s // TM,),
        in_specs=[pl.BlockSpec((1, 1, TOP_K * TM), lambda i: (i, 0, 0), memory_space=pltpu.SMEM),
                  row(D_MODEL), row(LANES), pl.BlockSpec(memory_space=pl.ANY)],
        out_specs=row(D_MODEL),
        out_shape=jax.ShapeDtypeStruct((s, D_MODEL), F32),
        scratch_shapes=[pltpu.VMEM((TOP_K, TM, D_MODEL), F32), pltpu.SemaphoreType.DMA],
        compiler_params=_cparams(("arbitrary",)),
        name="moe_combine",
    )(pos.reshape(s // TM, 1, TOP_K * TM), x1, route, ys)


def _moe_layer(x1, h2, route, wg, wu, wd):
    s = h2.shape[0]
    n_blocks = s * TOP_K // MOE_BLOCK + N_EXPERTS
    flat_e = route[:, :TOP_K].astype(jnp.int32).reshape(-1)
    onehot = (flat_e[:, None] == jnp.arange(N_EXPERTS)[None, :]).astype(jnp.int32)
    csum = jnp.cumsum(onehot, axis=0)
    counts = csum[-1]
    padded = (counts + MOE_BLOCK - 1) // MOE_BLOCK * MOE_BLOCK
    pend = jnp.cumsum(padded)
    pstart = pend - padded
    pos = jnp.sum(onehot * (csum - 1 + pstart[None, :]), axis=1).astype(jnp.int32)
    blk_e = jnp.minimum(jnp.searchsorted(pend, jnp.arange(n_blocks) * MOE_BLOCK, side='right'),
                        N_EXPERTS - 1).astype(jnp.int32)
    n_used = (pend[-1] // MOE_BLOCK).astype(jnp.int32).reshape(1)
    fill_blk = jnp.maximum(pend // MOE_BLOCK - 1, 0).astype(jnp.int32)
    xs = _moe_dispatch(h2, pos, fill_blk, n_used, n_blocks)
    ys = _moe_experts(xs, blk_e, n_used, wg, wu, wd, n_blocks)
    return _moe_combine(x1, route, pos, ys)


def _block_diag_ones(width, group):
    idx = jnp.arange(width) // group
    return (idx[:, None] == idx[None, :]).astype(BF16)


def kernel(x, norm_mix, w_in, q_norm_sb, k_norm_sb, q_norm_ca, k_norm_ca, rel_bias, lam_re, lam_im, log_dt, b_re, b_im, c_re, c_im, d_skip, w_glu, g_out, w_out, norm_ffn, w_gate_dense, w_up_dense, w_down_dense, w_router, w_gate_moe, w_up_moe, w_down_moe):
    b, s, d = x.shape
    assert b == 1 and d == D_MODEL and s % (TM * 2) == 0
    xs = x.reshape(s, d).astype(F32)
    scale = 1.0 / math.sqrt(HEAD_DIM)
    gm = _block_diag_ones(CA_WIDTH, HEAD_DIM)
    idx = jnp.arange(SB_BLOCK)
    tri = (idx[:, None] > idx[None, :]).astype(BF16)
    vec = lambda a: a.reshape(1, -1).astype(F32)
    tile = lambda a, width: jnp.tile(a.astype(F32), width // HEAD_DIM).reshape(1, width)

    for l in range(DEPTH):
        q_sb, k_sb, v_sb, q_ca, k_ca, v_ca, u = _in_proj(
            xs, vec(norm_mix[l]), w_in[l].astype(BF16),
            tile(q_norm_sb[l], SB_WIDTH) * scale, tile(k_norm_sb[l], SB_WIDTH),
            tile(q_norm_ca[l], CA_WIDTH) * scale, tile(k_norm_ca[l], CA_WIDTH), gm)
        o_sb = _sb_attention(q_sb, k_sb, v_sb, tri)
        o_ca = _ca_attention(q_ca, k_ca, v_ca, _ca_bias_mask(rel_bias[l]))
        o_ssm = _s5_mixer(u, lam_re[l], lam_im[l], log_dt[l], b_re[l], b_im[l], c_re[l], c_im[l],
                          d_skip[l], w_glu[l])
        i = l // 2
        if l % 2 == 0:
            x1, h2 = _mix_out(o_sb, o_ca, o_ssm, xs, vec(g_out[l]), w_out[l].astype(BF16), vec(norm_ffn[l]))
            xs = _ffn_dense(h2, x1, w_gate_dense[i].astype(BF16), w_up_dense[i].astype(BF16),
                            w_down_dense[i].astype(BF16))
        else:
            wr = jnp.pad(w_router[i].astype(F32), ((0, 0), (0, LANES - N_EXPERTS)))
            r_hi, r_lo = _split_bf16(wr)
            x1, h2, route = _mix_out(o_sb, o_ca, o_ssm, xs, vec(g_out[l]), w_out[l].astype(BF16),
                                     vec(norm_ffn[l]), router=(r_hi, r_lo))
            xs = _moe_layer(x1, h2, route, w_gate_moe[i].astype(BF16), w_up_moe[i].astype(BF16),
                            w_down_moe[i].astype(BF16))
    return xs.reshape(b, s, d).astype(x.dtype)
```

```python
import functools
import math

import jax
import jax.numpy as jnp
import numpy as np
from jax import lax
from jax.experimental import pallas as pl
from jax.experimental.pallas import tpu as pltpu

F32 = jnp.float32
BF16 = jnp.bfloat16

D_MODEL = 1024
DEPTH = 4
CHUNK = 64
HEAD_DIM = 64
SB_WIDTH = 256
CA_WIDTH = 512
CA_HEADS = 8
SSM_WIDTH = 256
SSM_GROUP_CH = 16
SSM_GROUPS = 16
SSM_STATE = 64
N_STATES = SSM_GROUPS * SSM_STATE
D_IN_PROJ = 3 * SB_WIDTH + 3 * CA_WIDTH + SSM_WIDTH
MASK_LOGIT = -1e4
LOOKBACK = 8
REL_CLIP = 128
D_FF = 2816
N_EXPERTS = 8
TOP_K = 2
RMS_EPS = 1e-6

LANES = 128
SUBLANES = 8
VMEM_LIMIT = 56 * 1024 * 1024

TM = 512
SB_BLOCK = 256
CA_BLOCK = 256
MOE_BLOCK = 512
SB_UNDERFLOW = 104.0


def _cparams(sem):
    return pltpu.CompilerParams(dimension_semantics=sem, vmem_limit_bytes=VMEM_LIMIT)


def _split_bf16(x):
    hi = x.astype(BF16)
    lo = (x - hi.astype(F32)).astype(BF16)
    return hi, lo


def _rms(x, gain):
    ms = jnp.mean(x * x, axis=-1, keepdims=True)
    return x * lax.rsqrt(ms + RMS_EPS) * gain


GROUP_SUM_WIDTH = 256


def _head_rms(p, gmat, gain):
    hi, lo = _split_bf16(p * p)
    blocks = []
    for c in range(p.shape[1] // GROUP_SUM_WIDTH):
        cols = slice(c * GROUP_SUM_WIDTH, (c + 1) * GROUP_SUM_WIDTH)
        blocks.append(jnp.dot(hi[:, cols], gmat, preferred_element_type=F32)
                      + jnp.dot(lo[:, cols], gmat, preferred_element_type=F32))
    ss = blocks[0] if len(blocks) == 1 else jnp.concatenate(blocks, axis=1)
    return p * lax.rsqrt(ss * (1.0 / HEAD_DIM) + RMS_EPS) * gain


def _in_proj_kernel(x_ref, g_ref, w_ref, gq_sb, gk_sb, gq_ca, gk_ca, gm_ref,
                    qsb_o, ksb_o, vsb_o, qca_o, kca_o, vca_o, u_o):
    h = _rms(x_ref[...], g_ref[...])
    proj = jnp.dot(h.astype(BF16), w_ref[...], preferred_element_type=F32)
    gm = gm_ref[...]
    o1, o2, o3 = SB_WIDTH, 2 * SB_WIDTH, 3 * SB_WIDTH
    o4, o5, o6 = o3 + CA_WIDTH, o3 + 2 * CA_WIDTH, o3 + 3 * CA_WIDTH
    qsb_o[...] = _head_rms(proj[:, :o1], gm, gq_sb[...]).astype(BF16)
    ksb_o[...] = _head_rms(proj[:, o1:o2], gm, gk_sb[...]).astype(BF16)
    vsb_o[...] = proj[:, o2:o3].astype(BF16)
    qca_o[...] = _head_rms(proj[:, o3:o4], gm, gq_ca[...]).astype(BF16)
    kca_o[...] = _head_rms(proj[:, o4:o5], gm, gk_ca[...]).astype(BF16)
    vca_o[...] = proj[:, o5:o6].astype(BF16)
    u_o[...] = proj[:, o6:]


def _in_proj(x, g, w, gq_sb, gk_sb, gq_ca, gk_ca, gm):
    s = x.shape[0]
    row = lambda w_: pl.BlockSpec((TM, w_), lambda i: (i, 0))
    full = lambda a: pl.BlockSpec(a.shape, lambda i: (0,) * a.ndim)
    return pl.pallas_call(
        _in_proj_kernel,
        grid=(s // TM,),
        in_specs=[row(D_MODEL)] + [full(a) for a in (g, w, gq_sb, gk_sb, gq_ca, gk_ca, gm)],
        out_specs=[row(SB_WIDTH)] * 3 + [row(CA_WIDTH)] * 3 + [row(SSM_WIDTH)],
        out_shape=[jax.ShapeDtypeStruct((s, SB_WIDTH), BF16)] * 3
        + [jax.ShapeDtypeStruct((s, CA_WIDTH), BF16)] * 3
        + [jax.ShapeDtypeStruct((s, SSM_WIDTH), F32)],
        compiler_params=_cparams(("parallel",)),
        name="in_proj",
    )(x, g, w, gq_sb, gk_sb, gq_ca, gk_ca, gm)


def _sb_kernel(q_ref, k_ref, v_ref, tri_ref, o_ref):
    i = pl.program_id(0)
    tri = tri_ref[...]
    row = lax.broadcasted_iota(jnp.int32, (SB_BLOCK, SB_BLOCK), 0)
    col = lax.broadcasted_iota(jnp.int32, (SB_BLOCK, SB_BLOCK), 1)
    rel = col - row
    lane = lax.broadcasted_iota(jnp.int32, (SB_BLOCK, LANES), 1)
    n_pairs = SB_WIDTH // LANES
    per_pair = LANES // HEAD_DIM
    in_head = [(lane >= hh * HEAD_DIM) & (lane < (hh + 1) * HEAD_DIM) for hh in range(per_pair)]
    qm = [jnp.where(in_head[hh], q_ref[:, p * LANES:(p + 1) * LANES].astype(F32), 0.0).astype(BF16)
          for p in range(n_pairs) for hh in range(per_pair)]

    def cond(c):
        j, done = c[0], c[1]
        return jnp.logical_and(j >= 0, jnp.logical_not(done))

    def body(c):
        j, _, carries, accs = c
        start = pl.multiple_of(j * SB_BLOCK, SB_BLOCK)
        causal = rel < (i - j) * SB_BLOCK
        heads = range(n_pairs * per_pair)
        lanes = [slice(p * LANES, (p + 1) * LANES) for p in range(n_pairs)]
        kb = [k_ref[pl.ds(start, SB_BLOCK), lanes[p]] for p in range(n_pairs)]
        vb = [v_ref[pl.ds(start, SB_BLOCK), lanes[p]] for p in range(n_pairs)]
        z = [lax.dot_general(qm[h], kb[h // per_pair], (((1,), (1,)), ((), ())), preferred_element_type=F32)
             for h in heads]
        z = [jnp.where(causal, z[h], MASK_LOGIT) for h in heads]
        sp = [jnp.maximum(z[h], 0.0) + jnp.log(1.0 + jnp.exp(-jnp.abs(z[h]))) for h in heads]
        split = [_split_bf16(sp[h]) for h in heads]
        later = [jnp.dot(split[h][0], tri, preferred_element_type=F32)
                 + jnp.dot(split[h][1], tri, preferred_element_type=F32) for h in heads]
        w = [jnp.exp(z[h] - sp[h] - (later[h] + carries[h])).astype(BF16) for h in heads]
        pv = [jnp.dot(w[h], vb[h // per_pair], preferred_element_type=F32) for h in heads]
        new_accs = [accs[p] + functools.reduce(
            lambda a, b_: a + b_,
            [jnp.where(in_head[hh], pv[p * per_pair + hh], 0.0) for hh in range(per_pair)])
            for p in range(n_pairs)]
        new_carries = [carries[h] + jnp.sum(sp[h], axis=-1, keepdims=True) for h in heads]
        done = jnp.min(functools.reduce(jnp.minimum, new_carries)) > SB_UNDERFLOW
        return j - 1, done, tuple(new_carries), tuple(new_accs)

    init = (i, False,
            tuple(jnp.zeros((SB_BLOCK, 1), F32) for _ in range(n_pairs * per_pair)),
            tuple(jnp.zeros((SB_BLOCK, LANES), F32) for _ in range(n_pairs)))
    accs = lax.while_loop(cond, body, init)[3]
    for p in range(n_pairs):
        o_ref[:, p * LANES:(p + 1) * LANES] = accs[p]


def _sb_attention(q, k, v, tri):
    s = q.shape[0]
    whole = pl.BlockSpec(memory_space=pltpu.VMEM)
    return pl.pallas_call(
        _sb_kernel,
        grid=(s // SB_BLOCK,),
        in_specs=[pl.BlockSpec((SB_BLOCK, SB_WIDTH), lambda i: (i, 0)), whole, whole, whole],
        out_specs=pl.BlockSpec((SB_BLOCK, SB_WIDTH), lambda i: (i, 0)),
        out_shape=jax.ShapeDtypeStruct((s, SB_WIDTH), F32),
        compiler_params=_cparams(("parallel",)),
        name="sb_attention",
    )(q, k, v, tri)


CA_WINDOW = 3
CA_PAIRS_PER_GROUP = 2


def _ca_kernel(q_ref, k0, k1, k2, v0, v1, v2, bm_ref, o_ref):
    i = pl.program_id(0)
    k_refs = (k0, k1, k2)
    v_refs = (v0, v1, v2)
    lane = lax.broadcasted_iota(jnp.int32, (CA_BLOCK, LANES), 1)
    per_pair = LANES // HEAD_DIM
    in_head = [(lane >= hh * HEAD_DIM) & (lane < (hh + 1) * HEAD_DIM) for hh in range(per_pair)]
    add = lambda a, b_: a + b_
    tiles = range(CA_WINDOW)
    for group in range(CA_WIDTH // LANES // CA_PAIRS_PER_GROUP):
        pairs = [group * CA_PAIRS_PER_GROUP + g for g in range(CA_PAIRS_PER_GROUP)]
        heads = [(p, hh) for p in pairs for hh in range(per_pair)]
        lanes = {p: slice(p * LANES, (p + 1) * LANES) for p in pairs}
        qm = {(p, hh): jnp.where(in_head[hh], q_ref[:, lanes[p]].astype(F32), 0.0).astype(BF16)
              for p, hh in heads}
        sc = {(hd, jj): lax.dot_general(qm[hd], k_refs[jj][:, lanes[hd[0]]], (((1,), (1,)), ((), ())),
                                        preferred_element_type=F32)
              for hd in heads for jj in tiles}
        sc = {(hd, jj): jnp.where(i + jj >= CA_WINDOW - 1,
                                  sc[hd, jj] + bm_ref[hd[0] * per_pair + hd[1], :, jj * CA_BLOCK:(jj + 1) * CA_BLOCK],
                                  -1e30)
              for hd in heads for jj in tiles}
        m = {hd: functools.reduce(jnp.maximum, [jnp.max(sc[hd, jj], axis=-1, keepdims=True) for jj in tiles])
             for hd in heads}
        p_un = {(hd, jj): jnp.exp(sc[hd, jj] - m[hd]) for hd in heads for jj in tiles}
        denom = {hd: functools.reduce(add, [jnp.sum(p_un[hd, jj], axis=-1, keepdims=True) for jj in tiles])
                 for hd in heads}
        pv = {hd: functools.reduce(add, [jnp.dot(p_un[hd, jj].astype(BF16), v_refs[jj][:, lanes[hd[0]]],
                                                 preferred_element_type=F32) for jj in tiles])
              for hd in heads}
        for p in pairs:
            o_ref[:, lanes[p]] = functools.reduce(
                add, [jnp.where(in_head[hh], pv[p, hh] / denom[p, hh], 0.0) for hh in range(per_pair)])


def _ca_attention(q, k, v, bm, layer):
    s = q.shape[0]
    blk = lambda off: pl.BlockSpec((CA_BLOCK, CA_WIDTH), lambda i: (jnp.maximum(i - off, 0), 0))
    return pl.pallas_call(
        _ca_kernel,
        grid=(s // CA_BLOCK,),
        in_specs=[blk(0), blk(2), blk(1), blk(0), blk(2), blk(1), blk(0),
                  pl.BlockSpec((CA_HEADS,) + bm.shape[1:], lambda i: (layer, 0, 0))],
        out_specs=blk(0),
        out_shape=jax.ShapeDtypeStruct((s, CA_WIDTH), F32),
        compiler_params=_cparams(("parallel",)),
        name="ca_attention",
    )(q, k, k, k, v, v, v, bm)


def _ca_bias_mask(rel_bias):
    width = CA_WINDOW * CA_BLOCK
    period = CA_BLOCK + width - 1
    m = (np.arange(period) + CA_BLOCK - 1) % period
    dist = LOOKBACK * CHUNK + (CA_BLOCK - 1) - m
    by_offset = rel_bias.astype(F32)[:, np.clip(dist, -REL_CLIP, REL_CLIP) + REL_CLIP]
    toeplitz = jnp.tile(by_offset, (1, CA_BLOCK))[:, :CA_BLOCK * (period - 1)]
    toeplitz = toeplitz.reshape(-1, CA_BLOCK, period - 1)[:, :, :width]
    r = np.arange(CA_BLOCK)[:, None]
    c = np.arange(width)[None, :]
    back = r // CHUNK - (c // CHUNK - LOOKBACK)
    valid = (back >= 0) & (back <= LOOKBACK)
    return jnp.where(jnp.asarray(valid)[None], toeplitz, -1e30)


def _s5_disc_kernel(lre_ref, lim_ref, ldt_ref, bre_ref, bim_ref, are_o, aim_o, bbre_o, bbim_o):
    lre = lre_ref[...]
    lim = lim_ref[...]
    dt = jnp.exp(ldt_ref[...])
    mag = jnp.exp(lre * dt)
    ang = lim * dt
    a_re = mag * jnp.cos(ang)
    a_im = mag * jnp.sin(ang)
    are_o[...] = a_re
    aim_o[...] = a_im
    nre = a_re - 1.0
    nim = a_im
    den = lre * lre + lim * lim
    zoh_re = (nre * lre + nim * lim) / den
    zoh_im = (nim * lre - nre * lim) / den
    bre = bre_ref[...]
    bim = bim_ref[...]
    bbre_o[...] = zoh_re * bre - zoh_im * bim
    bbim_o[...] = zoh_re * bim + zoh_im * bre


def _s5_discretize(lam_re, lam_im, log_dt, b_re, b_im):
    col = lambda a: a.reshape(N_STATES, 1).astype(F32)
    ldt = jnp.repeat(log_dt.astype(F32), SSM_STATE).reshape(N_STATES, 1)
    b2 = lambda a: a.reshape(N_STATES, SSM_GROUP_CH).astype(F32)
    return pl.pallas_call(
        _s5_disc_kernel,
        out_shape=[jax.ShapeDtypeStruct((N_STATES, 1), F32)] * 2
        + [jax.ShapeDtypeStruct((N_STATES, SSM_GROUP_CH), F32)] * 2,
        name="s5_discretize",
    )(col(lam_re), col(lam_im), ldt, b2(b_re), b2(b_im))


S5_SEGS = SUBLANES
S5_STEPS = 64


def _cmul(a_re, a_im, b_re, b_im):
    return a_re * b_re - a_im * b_im, a_re * b_im + a_im * b_re


def _s5_local_kernel(u_ref, wre_ref, wim_ref, are_ref, aim_ref, cre_ref, cim_ref, y_o, end_o,
                     xre, xim, st):
    @pl.when(pl.program_id(0) == 0)
    def _():
        st[...] = jnp.zeros_like(st)

    u = u_ref[...].astype(BF16)
    xre[...] = jnp.dot(u, wre_ref[...], preferred_element_type=F32)
    xim[...] = jnp.dot(u, wim_ref[...], preferred_element_type=F32)
    a_re = jnp.broadcast_to(are_ref[...], (S5_SEGS, N_STATES))
    a_im = jnp.broadcast_to(aim_ref[...], (S5_SEGS, N_STATES))

    def step(k, c):
        rows = pl.ds(pl.multiple_of(k * S5_SEGS, S5_SEGS), S5_SEGS)
        p_re, p_im = _cmul(a_re, a_im, *c)
        n_re = p_re + xre[rows, :]
        n_im = p_im + xim[rows, :]
        xre[rows, :] = n_re
        xim[rows, :] = n_im
        return n_re, n_im

    x_re, x_im = lax.fori_loop(0, S5_STEPS, step, (st[0], st[1]), unroll=4)
    st[0] = x_re
    st[1] = x_im
    end_o[0] = x_re
    end_o[1] = x_im
    y_o[...] = (jnp.dot(xre[...].astype(BF16), cre_ref[...], preferred_element_type=F32)
                - jnp.dot(xim[...].astype(BF16), cim_ref[...], preferred_element_type=F32))


def _s5_finish_kernel(seg_len, y_ref, u_ref, end_ref, are_ref, aim_ref, cre_ref, cim_ref, d_ref, wglu_ref,
                      o_ref, zre, zim, x0, pw):
    a_re = jnp.broadcast_to(are_ref[...], (S5_SEGS, N_STATES))
    a_im = jnp.broadcast_to(aim_ref[...], (S5_SEGS, N_STATES))

    @pl.when(pl.program_id(0) == 0)
    def _():
        r_re, r_im = are_ref[...], aim_ref[...]
        p_re, p_im = jnp.ones_like(r_re), jnp.zeros_like(r_re)
        n = seg_len
        while n:
            if n & 1:
                p_re, p_im = _cmul(p_re, p_im, r_re, r_im)
            r_re, r_im = _cmul(r_re, r_im, r_re, r_im)
            n >>= 1
        rows_re, rows_im = [jnp.zeros_like(p_re)], [jnp.zeros_like(p_re)]
        for s in range(1, S5_SEGS):
            c_re, c_im = _cmul(p_re, p_im, rows_re[-1], rows_im[-1])
            rows_re.append(c_re + end_ref[0, s - 1:s, :])
            rows_im.append(c_im + end_ref[1, s - 1:s, :])
        x0[0] = jnp.concatenate(rows_re, axis=0)
        x0[1] = jnp.concatenate(rows_im, axis=0)
        pw[0] = a_re
        pw[1] = a_im

    x0_re = x0[0]
    x0_im = x0[1]

    def step(k, c):
        rows = pl.ds(pl.multiple_of(k * S5_SEGS, S5_SEGS), S5_SEGS)
        z_re, z_im = _cmul(c[0], c[1], x0_re, x0_im)
        zre[rows, :] = z_re
        zim[rows, :] = z_im
        return _cmul(a_re, a_im, *c)

    p_re, p_im = lax.fori_loop(0, S5_STEPS, step, (pw[0], pw[1]), unroll=4)
    pw[0] = p_re
    pw[1] = p_im
    y = y_ref[...] + (jnp.dot(zre[...].astype(BF16), cre_ref[...], preferred_element_type=F32)
                      - jnp.dot(zim[...].astype(BF16), cim_ref[...], preferred_element_type=F32))
    y = y + d_ref[...] * u_ref[...]
    y = 0.5 * y * (1.0 + jnp.tanh(math.sqrt(2.0 / math.pi) * (y + 0.044715 * (y * y * y))))
    hg = jnp.dot(y.astype(BF16), wglu_ref[...], preferred_element_type=F32)
    o_ref[...] = hg[:, :SSM_WIDTH] / (1.0 + jnp.exp(-hg[:, SSM_WIDTH:]))


def _s5_mixer(u, lam_re, lam_im, log_dt, b_re, b_im, c_re, c_im, d_skip, w_glu):
    s = u.shape[0]
    seg_len = s // S5_SEGS
    rows = S5_STEPS * S5_SEGS
    a_re, a_im, bb_re, bb_im = _s5_discretize(lam_re, lam_im, log_dt, b_re, b_im)
    a_re = a_re.reshape(1, N_STATES)
    a_im = a_im.reshape(1, N_STATES)
    eye = jnp.eye(SSM_GROUPS, dtype=F32)

    def in_matrix(bb):
        bb = bb.reshape(SSM_GROUPS, SSM_STATE, SSM_GROUP_CH)
        return jnp.einsum('gpc,gh->gchp', bb, eye).reshape(SSM_WIDTH, N_STATES).astype(BF16)

    def out_matrix(c):
        return jnp.einsum('gcp,gh->gphc', c.astype(F32), eye).reshape(N_STATES, SSM_WIDTH).astype(BF16)

    w_re, w_im = in_matrix(bb_re), in_matrix(bb_im)
    cm_re, cm_im = out_matrix(c_re), out_matrix(c_im)
    d = d_skip.reshape(1, SSM_WIDTH).astype(F32)
    wg = w_glu.astype(BF16)
    u_seg = u.reshape(S5_SEGS, seg_len, SSM_WIDTH).transpose(1, 0, 2).reshape(s, SSM_WIDTH)
    full = lambda a: pl.BlockSpec(a.shape, lambda i: (0,) * a.ndim)
    row = pl.BlockSpec((rows, SSM_WIDTH), lambda i: (i, 0))
    ends = pl.BlockSpec((2, S5_SEGS, N_STATES), lambda i: (0, 0, 0))
    big = pltpu.VMEM((rows, N_STATES), F32)
    small = pltpu.VMEM((2, S5_SEGS, N_STATES), F32)
    y_local, end = pl.pallas_call(
        _s5_local_kernel,
        grid=(seg_len // S5_STEPS,),
        in_specs=[row] + [full(a) for a in (w_re, w_im, a_re, a_im, cm_re, cm_im)],
        out_specs=[row, ends],
        out_shape=[jax.ShapeDtypeStruct((s, SSM_WIDTH), F32),
                   jax.ShapeDtypeStruct((2, S5_SEGS, N_STATES), F32)],
        scratch_shapes=[big, big, small],
        compiler_params=_cparams(("arbitrary",)),
        name="s5_local",
    )(u_seg, w_re, w_im, a_re, a_im, cm_re, cm_im)
    o_seg = pl.pallas_call(
        functools.partial(_s5_finish_kernel, seg_len),
        grid=(seg_len // S5_STEPS,),
        in_specs=[row, row, ends] + [full(a) for a in (a_re, a_im, cm_re, cm_im, d, wg)],
        out_specs=row,
        out_shape=jax.ShapeDtypeStruct((s, SSM_WIDTH), F32),
        scratch_shapes=[big, big, small, small],
        compiler_params=_cparams(("arbitrary",)),
        name="s5_finish",
    )(y_local, u_seg, end, a_re, a_im, cm_re, cm_im, d, wg)
    return o_seg.reshape(seg_len, S5_SEGS, SSM_WIDTH).transpose(1, 0, 2).reshape(s, SSM_WIDTH)


def _mix_core(osb_ref, oca_ref, ossm_ref, x_ref, g_ref, w_ref, nf_ref):
    g = g_ref[...]
    mixed = jnp.concatenate([
        _rms(osb_ref[...], g[:, :SB_WIDTH]),
        _rms(oca_ref[...], g[:, SB_WIDTH:SB_WIDTH + CA_WIDTH]),
        _rms(ossm_ref[...], g[:, SB_WIDTH + CA_WIDTH:]),
    ], axis=-1)
    x1 = x_ref[...] + jnp.dot(mixed.astype(BF16), w_ref[...], preferred_element_type=F32)
    return x1, _rms(x1, nf_ref[...])


def _mix_dense_kernel(osb_ref, oca_ref, ossm_ref, x_ref, g_ref, w_ref, nf_ref, x1_o, h2_o):
    x1, h2 = _mix_core(osb_ref, oca_ref, ossm_ref, x_ref, g_ref, w_ref, nf_ref)
    x1_o[...] = x1
    h2_o[...] = h2.astype(BF16)


def _mix_moe_kernel(osb_ref, oca_ref, ossm_ref, x_ref, g_ref, w_ref, nf_ref, rboth_ref, rhi_ref,
                    x1_o, h2_o, route_o):
    x1, h2 = _mix_core(osb_ref, oca_ref, ossm_ref, x_ref, g_ref, w_ref, nf_ref)
    x1_o[...] = x1
    h2_o[...] = h2
    hi, lo = _split_bf16(h2)
    nt = (((1,), (1,)), ((), ()))
    both = lax.dot_general(rboth_ref[...], hi, nt, preferred_element_type=F32)
    cross = lax.dot_general(rhi_ref[...], lo, nt, preferred_element_type=F32)
    lg = both[:N_EXPERTS] + both[N_EXPERTS:] + cross[:N_EXPERTS]
    idx = lax.broadcasted_iota(jnp.int32, lg.shape, 0)
    neg = -jnp.inf
    m1 = jnp.max(lg, axis=0, keepdims=True)
    i1 = jnp.min(jnp.where(lg == m1, idx, N_EXPERTS), axis=0, keepdims=True)
    lg2 = jnp.where(idx == i1, neg, lg)
    m2 = jnp.max(lg2, axis=0, keepdims=True)
    i2 = jnp.min(jnp.where(lg2 == m2, idx, N_EXPERTS), axis=0, keepdims=True)
    e = jnp.exp(m2 - m1)
    g1 = 1.0 / (1.0 + e)
    g2 = e / (1.0 + e)
    route_o[...] = jnp.where(idx == 0, i1.astype(F32),
                             jnp.where(idx == 1, i2.astype(F32),
                                       jnp.where(idx == 2, g1, jnp.where(idx == 3, g2, 0.0))))


def _mix_out(o_sb, o_ca, o_ssm, x, g, w, nf, router=None):
    s = x.shape[0]
    row = lambda w_: pl.BlockSpec((TM, w_), lambda i: (i, 0))
    full = lambda a: pl.BlockSpec(a.shape, lambda i: (0, 0))
    ins = [o_sb, o_ca, o_ssm, x, g, w, nf]
    in_specs = [row(SB_WIDTH), row(CA_WIDTH), row(SSM_WIDTH), row(D_MODEL), full(g), full(w), full(nf)]
    if router is None:
        return pl.pallas_call(
            _mix_dense_kernel, grid=(s // TM,), in_specs=in_specs,
            out_specs=[row(D_MODEL)] * 2,
            out_shape=[jax.ShapeDtypeStruct((s, D_MODEL), F32), jax.ShapeDtypeStruct((s, D_MODEL), BF16)],
            compiler_params=_cparams(("parallel",)), name="mix_out_dense",
        )(*ins)
    return pl.pallas_call(
        _mix_moe_kernel, grid=(s // TM,), in_specs=in_specs + [full(router[0]), full(router[1])],
        out_specs=[row(D_MODEL), row(D_MODEL), pl.BlockSpec((N_EXPERTS, TM), lambda i: (0, i))],
        out_shape=[jax.ShapeDtypeStruct((s, D_MODEL), F32)] * 2 + [jax.ShapeDtypeStruct((N_EXPERTS, s), F32)],
        compiler_params=_cparams(("parallel",)), name="mix_out_moe",
    )(*ins, *router)


def _swiglu_tile(xb, wg, wu, wd):
    hg = jnp.dot(xb, wg, preferred_element_type=F32)
    hu = jnp.dot(xb, wu, preferred_element_type=F32)
    hid = hg / (1.0 + jnp.exp(-hg)) * hu
    return jnp.dot(hid.astype(BF16), wd, preferred_element_type=F32)


FF_CHUNKS = (1024, 1024, 768)


def _swiglu_chunks(xb, wg_ref, wu_ref, wd_ref):
    assert sum(FF_CHUNKS) == D_FF
    y = None
    start = 0
    for width in FF_CHUNKS:
        cols = slice(start, start + width)
        part = _swiglu_tile(xb, wg_ref[:, cols], wu_ref[:, cols], wd_ref[cols, :])
        y = part if y is None else y + part
        start += width
    return y


def _ffn_kernel(h_ref, x_ref, wg_ref, wu_ref, wd_ref, o_ref):
    o_ref[...] = x_ref[...] + _swiglu_chunks(h_ref[...], wg_ref.at[0], wu_ref.at[0], wd_ref.at[0])


def _ffn_dense(h2, x1, wg, wu, wd, layer):
    s = x1.shape[0]
    row = pl.BlockSpec((TM, D_MODEL), lambda i: (i, 0))
    resident = lambda a: pl.BlockSpec((1,) + a.shape[1:], lambda i: (layer, 0, 0), pipeline_mode=pl.Buffered(1))
    return pl.pallas_call(
        _ffn_kernel,
        grid=(s // TM,),
        in_specs=[row, row, resident(wg), resident(wu), resident(wd)],
        out_specs=row,
        out_shape=jax.ShapeDtypeStruct((s, D_MODEL), F32),
        compiler_params=_cparams(("parallel",)),
        name="ffn_dense",
    )(h2, x1, wg, wu, wd)


def _for_rows(n, fn):
    for r in range(n):
        fn(r)


def _dispatch_kernel(fill_blk, n_used, pos_ref, h_ref, xs_hbm, zbuf, sem, zsem):
    n_blocks = xs_hbm.shape[0] // MOE_BLOCK

    def zero_fill(blk):
        return pltpu.make_async_copy(
            zbuf, xs_hbm.at[pl.ds(pl.multiple_of(blk * MOE_BLOCK, MOE_BLOCK), MOE_BLOCK)], zsem)

    @pl.when(pl.program_id(0) == 0)
    def _():
        zbuf[...] = jnp.zeros_like(zbuf)
        def fill(blk):
            cp = zero_fill(blk)
            cp.start()
            cp.wait()

        for e in range(N_EXPERTS):
            fill(fill_blk[e])

        def tail(blk, c):
            fill(blk)
            return c
        lax.fori_loop(n_used[0], n_blocks, tail, 0)

    def row_copy(r, k):
        return pltpu.make_async_copy(h_ref.at[pl.ds(r, 1)], xs_hbm.at[pl.ds(pos_ref[0, 0, TOP_K * r + k], 1)], sem)

    def start(r):
        for k in range(TOP_K):
            row_copy(r, k).start(priority=k)

    def wait(r):
        for k in range(TOP_K):
            row_copy(r, k).wait()

    _for_rows(TM, start)
    _for_rows(TM, wait)


def _moe_dispatch(h2, pos, fill_blk, n_used, n_blocks):
    s = h2.shape[0]
    grid_spec = pltpu.PrefetchScalarGridSpec(
        num_scalar_prefetch=2,
        grid=(s // TM,),
        in_specs=[pl.BlockSpec((1, 1, TOP_K * TM), lambda i, f, u: (i, 0, 0), memory_space=pltpu.SMEM),
                  pl.BlockSpec((TM, D_MODEL), lambda i, f, u: (i, 0))],
        out_specs=pl.BlockSpec(memory_space=pl.ANY),
        scratch_shapes=[pltpu.VMEM((MOE_BLOCK, D_MODEL), F32), pltpu.SemaphoreType.DMA,
                        pltpu.SemaphoreType.DMA],
    )
    return pl.pallas_call(
        _dispatch_kernel,
        grid_spec=grid_spec,
        out_shape=jax.ShapeDtypeStruct((n_blocks * MOE_BLOCK, D_MODEL), F32),
        compiler_params=_cparams(("arbitrary",)),
        name="moe_dispatch",
    )(fill_blk, n_used, pos.reshape(s // TM, 1, TOP_K * TM), h2)


def _moe_kernel(blk_e, n_used, xs_ref, wg_ref, wu_ref, wd_ref, o_ref):
    b = pl.program_id(0)

    @pl.when(b < n_used[0])
    def _():
        o_ref[...] = _swiglu_chunks(xs_ref[...].astype(BF16), wg_ref.at[0, 0], wu_ref.at[0, 0], wd_ref.at[0, 0])

    @pl.when(b >= n_used[0])
    def _():
        o_ref[...] = jnp.zeros_like(o_ref)


def _moe_experts(xs, blk_e, n_used, wg, wu, wd, layer, n_blocks):
    weights = lambda a: pl.BlockSpec((1, 1) + a.shape[2:], lambda b, e, u: (layer, e[b], 0, 0))
    grid_spec = pltpu.PrefetchScalarGridSpec(
        num_scalar_prefetch=2,
        grid=(n_blocks,),
        in_specs=[pl.BlockSpec((MOE_BLOCK, D_MODEL), lambda b, e, u: (jnp.minimum(b, u[0] - 1), 0)),
                  weights(wg), weights(wu), weights(wd)],
        out_specs=pl.BlockSpec((MOE_BLOCK, D_MODEL), lambda b, e, u: (b, 0)),
    )
    return pl.pallas_call(
        _moe_kernel,
        grid_spec=grid_spec,
        out_shape=jax.ShapeDtypeStruct((n_blocks * MOE_BLOCK, D_MODEL), F32),
        compiler_params=_cparams(("arbitrary",)),
        name="moe_experts",
    )(blk_e, n_used, xs, wg, wu, wd)


def _combine_kernel(pos_ref, x_ref, gate_ref, ys_hbm, o_ref, ybuf, sem):
    def row_copy(r, k):
        return pltpu.make_async_copy(ys_hbm.at[pl.ds(pos_ref[0, 0, TOP_K * r + k], 1)],
                                     ybuf.at[k, pl.ds(r, 1)], sem)

    def start(r):
        for k in range(TOP_K):
            row_copy(r, k).start(priority=k)

    def wait(r):
        for k in range(TOP_K):
            row_copy(r, k).wait()

    _for_rows(TM, start)
    _for_rows(TM, wait)
    gates = gate_ref[...]
    gated = [ybuf[k] * gates[:, k:k + 1] for k in range(TOP_K)]
    o_ref[...] = x_ref[...] + functools.reduce(lambda a, b_: a + b_, gated)


def _moe_combine(x1, gates, pos, ys):
    s = x1.shape[0]
    row = lambda w_: pl.BlockSpec((TM, w_), lambda i: (i, 0))
    return pl.pallas_call(
        _combine_kernel,
        grid=(s // TM,),
        in_specs=[pl.BlockSpec((1, 1, TOP_K * TM), lambda i: (i, 0, 0), memory_space=pltpu.SMEM),
                  row(D_MODEL), row(TOP_K), pl.BlockSpec(memory_space=pl.ANY)],
        out_specs=row(D_MODEL),
        out_shape=jax.ShapeDtypeStruct((s, D_MODEL), F32),
        scratch_shapes=[pltpu.VMEM((TOP_K, TM, D_MODEL), F32), pltpu.SemaphoreType.DMA],
        compiler_params=_cparams(("arbitrary",)),
        name="moe_combine",
    )(pos.reshape(s // TM, 1, TOP_K * TM), x1, gates, ys)


def _moe_layer(x1, h2, route, wg, wu, wd, layer):
    s = h2.shape[0]
    n_blocks = s * TOP_K // MOE_BLOCK + N_EXPERTS
    flat_e = route[:TOP_K].T.astype(jnp.int32).reshape(-1)
    gates = route[TOP_K:2 * TOP_K].T
    onehot = (flat_e[:, None] == jnp.arange(N_EXPERTS)[None, :]).astype(jnp.int32)
    csum = jnp.cumsum(onehot, axis=0)
    counts = csum[-1]
    padded = (counts + MOE_BLOCK - 1) // MOE_BLOCK * MOE_BLOCK
    pend = jnp.cumsum(padded)
    pstart = pend - padded
    pos = jnp.sum(onehot * (csum - 1 + pstart[None, :]), axis=1).astype(jnp.int32)
    blk_start = jnp.arange(n_blocks, dtype=jnp.int32) * MOE_BLOCK
    blk_e = jnp.minimum(jnp.sum((pend[None, :] <= blk_start[:, None]).astype(jnp.int32), axis=1),
                        N_EXPERTS - 1).astype(jnp.int32)
    n_used = (pend[-1] // MOE_BLOCK).astype(jnp.int32).reshape(1)
    fill_blk = jnp.maximum(pend // MOE_BLOCK - 1, 0).astype(jnp.int32)
    xs = _moe_dispatch(h2, pos, fill_blk, n_used, n_blocks)
    ys = _moe_experts(xs, blk_e, n_used, wg, wu, wd, layer, n_blocks)
    return _moe_combine(x1, gates, pos, ys)


def _block_diag_ones(width, group):
    idx = jnp.arange(width) // group
    return (idx[:, None] == idx[None, :]).astype(BF16)


def kernel(x, norm_mix, w_in, q_norm_sb, k_norm_sb, q_norm_ca, k_norm_ca, rel_bias, lam_re, lam_im, log_dt, b_re, b_im, c_re, c_im, d_skip, w_glu, g_out, w_out, norm_ffn, w_gate_dense, w_up_dense, w_down_dense, w_router, w_gate_moe, w_up_moe, w_down_moe):
    b, s, d = x.shape
    assert b == 1 and d == D_MODEL and s % (TM * 2) == 0
    xs = x.reshape(s, d).astype(F32)
    scale = 1.0 / math.sqrt(HEAD_DIM)
    gm = _block_diag_ones(GROUP_SUM_WIDTH, HEAD_DIM)
    idx = jnp.arange(SB_BLOCK)
    tri = (idx[:, None] > idx[None, :]).astype(BF16)
    vec = lambda a: a.reshape(1, -1).astype(F32)
    tile = lambda a, width: jnp.tile(a.astype(F32), width // HEAD_DIM).reshape(1, width)

    dense_w = [w.astype(BF16) for w in (w_gate_dense, w_up_dense, w_down_dense)]
    moe_w = [w.astype(BF16) for w in (w_gate_moe, w_up_moe, w_down_moe)]
    ca_bias = _ca_bias_mask(rel_bias.reshape(DEPTH * CA_HEADS, 2 * REL_CLIP + 1))

    for l in range(DEPTH):
        q_sb, k_sb, v_sb, q_ca, k_ca, v_ca, u = _in_proj(
            xs, vec(norm_mix[l]), w_in[l].astype(BF16),
            tile(q_norm_sb[l], SB_WIDTH) * scale, tile(k_norm_sb[l], SB_WIDTH),
            tile(q_norm_ca[l], CA_WIDTH) * scale, tile(k_norm_ca[l], CA_WIDTH), gm)
        o_sb = _sb_attention(q_sb, k_sb, v_sb, tri)
        o_ca = _ca_attention(q_ca, k_ca, v_ca, ca_bias, layer=l)
        o_ssm = _s5_mixer(u, lam_re[l], lam_im[l], log_dt[l], b_re[l], b_im[l], c_re[l], c_im[l],
                          d_skip[l], w_glu[l])
        i = l // 2
        if l % 2 == 0:
            x1, h2 = _mix_out(o_sb, o_ca, o_ssm, xs, vec(g_out[l]), w_out[l].astype(BF16), vec(norm_ffn[l]))
            xs = _ffn_dense(h2, x1, *dense_w, layer=i)
        else:
            r_hi, r_lo = _split_bf16(w_router[i].astype(F32).T)
            r_both = jnp.concatenate([r_hi, r_lo], axis=0)
            r_hi = jnp.concatenate([r_hi, jnp.zeros_like(r_hi)], axis=0)
            x1, h2, route = _mix_out(o_sb, o_ca, o_ssm, xs, vec(g_out[l]), w_out[l].astype(BF16),
                                     vec(norm_ffn[l]), router=(r_both, r_hi))
            xs = _moe_layer(x1, h2, route, *moe_w, layer=i)
    return xs.reshape(b, s, d).astype(x.dtype)
```

```python
import functools
import math

import jax
import jax.numpy as jnp
import numpy as np
from jax import lax
from jax.experimental import pallas as pl
from jax.experimental.pallas import tpu as pltpu

F32 = jnp.float32
BF16 = jnp.bfloat16
MIX_DTYPE = BF16

D_MODEL = 1024
DEPTH = 4
CHUNK = 64
HEAD_DIM = 64
SB_WIDTH = 256
CA_WIDTH = 512
CA_HEADS = 8
SSM_WIDTH = 256
SSM_GROUP_CH = 16
SSM_GROUPS = 16
SSM_STATE = 64
N_STATES = SSM_GROUPS * SSM_STATE
D_IN_PROJ = 3 * SB_WIDTH + 3 * CA_WIDTH + SSM_WIDTH
MASK_LOGIT = -1e4
LOOKBACK = 8
REL_CLIP = 128
D_FF = 2816
N_EXPERTS = 8
TOP_K = 2
RMS_EPS = 1e-6

LANES = 128
SUBLANES = 8
VMEM_LIMIT = 56 * 1024 * 1024

TM = 512
SB_BLOCK = 256
SB_BLOCKS_PER_ITER = 2
CA_BLOCK = 256
MOE_BLOCK = 512
SB_UNDERFLOW = 104.0
LOG2E = math.log2(math.e)


def _cparams(sem):
    return pltpu.CompilerParams(dimension_semantics=sem, vmem_limit_bytes=VMEM_LIMIT)


def _split_bf16(x):
    hi = x.astype(BF16)
    lo = (x - hi.astype(F32)).astype(BF16)
    return hi, lo


def _rms(x, gain):
    ms = jnp.mean(x * x, axis=-1, keepdims=True)
    return x * lax.rsqrt(ms + RMS_EPS) * gain


GROUP_SUM_WIDTH = 256


def _head_rms(p, gmat, gain):
    sq = (p * p).astype(BF16)
    blocks = []
    for c in range(p.shape[1] // GROUP_SUM_WIDTH):
        cols = slice(c * GROUP_SUM_WIDTH, (c + 1) * GROUP_SUM_WIDTH)
        blocks.append(jnp.dot(sq[:, cols], gmat, preferred_element_type=F32))
    ss = blocks[0] if len(blocks) == 1 else jnp.concatenate(blocks, axis=1)
    return p * lax.rsqrt(ss * (1.0 / HEAD_DIM) + RMS_EPS) * gain


def _in_proj_kernel(x_ref, g_ref, w_ref, gq_sb, gk_sb, gq_ca, gk_ca, gm_ref,
                    qsb_o, ksb_o, vsb_o, qca_o, kca_o, vca_o, u_o):
    h = _rms(x_ref[...], g_ref[...])
    proj = jnp.dot(h.astype(BF16), w_ref[...], preferred_element_type=F32)
    gm = gm_ref[...]
    o1, o2, o3 = SB_WIDTH, 2 * SB_WIDTH, 3 * SB_WIDTH
    o4, o5, o6 = o3 + CA_WIDTH, o3 + 2 * CA_WIDTH, o3 + 3 * CA_WIDTH
    qsb_o[...] = _head_rms(proj[:, :o1], gm, gq_sb[...]).astype(BF16)
    ksb_o[...] = _head_rms(proj[:, o1:o2], gm, gk_sb[...]).astype(BF16)
    vsb_o[...] = proj[:, o2:o3].astype(BF16)
    qca_o[...] = _head_rms(proj[:, o3:o4], gm, gq_ca[...]).astype(BF16)
    kca_o[...] = _head_rms(proj[:, o4:o5], gm, gk_ca[...]).astype(BF16)
    vca_o[...] = proj[:, o5:o6].astype(BF16)
    u_o[...] = proj[:, o6:]


def _in_proj(x, g, w, gq_sb, gk_sb, gq_ca, gk_ca, gm):
    s = x.shape[0]
    row = lambda w_: pl.BlockSpec((TM, w_), lambda i: (i, 0))
    full = lambda a: pl.BlockSpec(a.shape, lambda i: (0,) * a.ndim)
    return pl.pallas_call(
        _in_proj_kernel,
        grid=(s // TM,),
        in_specs=[row(D_MODEL)] + [full(a) for a in (g, w, gq_sb, gk_sb, gq_ca, gk_ca, gm)],
        out_specs=[row(SB_WIDTH)] * 3 + [row(CA_WIDTH)] * 3 + [row(SSM_WIDTH)],
        out_shape=[jax.ShapeDtypeStruct((s, SB_WIDTH), BF16)] * 3
        + [jax.ShapeDtypeStruct((s, CA_WIDTH), BF16)] * 3
        + [jax.ShapeDtypeStruct((s, SSM_WIDTH), F32)],
        compiler_params=_cparams(("parallel",)),
        name="in_proj",
    )(x, g, w, gq_sb, gk_sb, gq_ca, gk_ca, gm)


def _sb_kernel(q_ref, k_ref, v_ref, tri_ref, o_ref):
    i = pl.program_id(0)
    tri = tri_ref[...]
    row = lax.broadcasted_iota(jnp.int32, (SB_BLOCK, SB_BLOCK), 0)
    col = lax.broadcasted_iota(jnp.int32, (SB_BLOCK, SB_BLOCK), 1)
    rel = col - row
    lane = lax.broadcasted_iota(jnp.int32, (SB_BLOCK, LANES), 1)
    n_pairs = SB_WIDTH // LANES
    per_pair = LANES // HEAD_DIM
    in_head = [(lane >= hh * HEAD_DIM) & (lane < (hh + 1) * HEAD_DIM) for hh in range(per_pair)]
    qm = [jnp.where(in_head[hh], q_ref[:, p * LANES:(p + 1) * LANES].astype(F32), 0.0).astype(BF16)
          for p in range(n_pairs) for hh in range(per_pair)]

    def cond(c):
        j, done = c[0], c[1]
        return jnp.logical_and(j >= 0, jnp.logical_not(done))

    def body(c):
        j, _, carries, accs = c
        heads = range(n_pairs * per_pair)
        lanes = [slice(p * LANES, (p + 1) * LANES) for p in range(n_pairs)]
        add = lambda a, b_: a + b_
        blocks = range(SB_BLOCKS_PER_ITER)
        starts = [pl.multiple_of(jnp.maximum(j - b_, 0) * SB_BLOCK, SB_BLOCK) for b_ in blocks]
        visible = [jnp.logical_and(rel < (i - j + b_) * SB_BLOCK, j - b_ >= 0) for b_ in blocks]
        kb = [[k_ref[pl.ds(starts[b_], SB_BLOCK), lanes[p]] for p in range(n_pairs)] for b_ in blocks]
        vb = [[v_ref[pl.ds(starts[b_], SB_BLOCK), lanes[p]] for p in range(n_pairs)] for b_ in blocks]
        tiles = [(b_, h) for b_ in blocks for h in heads]
        z = {t: lax.dot_general(qm[t[1]], kb[t[0]][t[1] // per_pair], (((1,), (1,)), ((), ())),
                                preferred_element_type=F32) for t in tiles}
        z = {t: jnp.where(visible[t[0]], z[t], MASK_LOGIT * LOG2E) for t in tiles}
        sp = {t: jnp.maximum(z[t], 0.0) + jnp.log2(1.0 + jnp.exp2(-jnp.abs(z[t]))) for t in tiles}
        later = {t: jnp.dot(sp[t].astype(BF16), tri, preferred_element_type=F32) for t in tiles}
        total = {t: jnp.sum(sp[t], axis=-1, keepdims=True) for t in tiles}
        carry = {}
        for h in heads:
            running = carries[h]
            for b_ in blocks:
                carry[b_, h] = running
                running = running + total[b_, h]
            carry[SB_BLOCKS_PER_ITER, h] = running
        w = {t: jnp.exp2(z[t] - sp[t] - (later[t] + carry[t])).astype(BF16) for t in tiles}
        pv = {t: jnp.dot(w[t], vb[t[0]][t[1] // per_pair], preferred_element_type=F32) for t in tiles}
        new_accs = [accs[p] + functools.reduce(
            add, [jnp.where(in_head[hh], functools.reduce(add, [pv[b_, p * per_pair + hh] for b_ in blocks]), 0.0)
                  for hh in range(per_pair)])
            for p in range(n_pairs)]
        new_carries = [carry[SB_BLOCKS_PER_ITER, h] for h in heads]
        done = jnp.min(functools.reduce(jnp.minimum, new_carries)) > SB_UNDERFLOW * LOG2E
        return j - SB_BLOCKS_PER_ITER, done, tuple(new_carries), tuple(new_accs)

    init = (i, False,
            tuple(jnp.zeros((SB_BLOCK, 1), F32) for _ in range(n_pairs * per_pair)),
            tuple(jnp.zeros((SB_BLOCK, LANES), F32) for _ in range(n_pairs)))
    accs = lax.while_loop(cond, body, init)[3]
    for p in range(n_pairs):
        o_ref[:, p * LANES:(p + 1) * LANES] = accs[p].astype(o_ref.dtype)


def _sb_attention(q, k, v, tri):
    s = q.shape[0]
    whole = pl.BlockSpec(memory_space=pltpu.VMEM)
    return pl.pallas_call(
        _sb_kernel,
        grid=(s // SB_BLOCK,),
        in_specs=[pl.BlockSpec((SB_BLOCK, SB_WIDTH), lambda i: (i, 0)), whole, whole, whole],
        out_specs=pl.BlockSpec((SB_BLOCK, SB_WIDTH), lambda i: (i, 0)),
        out_shape=jax.ShapeDtypeStruct((s, SB_WIDTH), MIX_DTYPE),
        compiler_params=_cparams(("parallel",)),
        name="sb_attention",
    )(q, k, v, tri)


CA_WINDOW = 3
CA_PAIRS_PER_GROUP = 2


def _ca_kernel(q_ref, k0, k1, k2, v0, v1, v2, bm_ref, o_ref):
    i = pl.program_id(0)
    k_refs = (k0, k1, k2)
    v_refs = (v0, v1, v2)
    lane = lax.broadcasted_iota(jnp.int32, (CA_BLOCK, LANES), 1)
    per_pair = LANES // HEAD_DIM
    in_head = [(lane >= hh * HEAD_DIM) & (lane < (hh + 1) * HEAD_DIM) for hh in range(per_pair)]
    add = lambda a, b_: a + b_
    tiles = range(CA_WINDOW)
    for group in range(CA_WIDTH // LANES // CA_PAIRS_PER_GROUP):
        pairs = [group * CA_PAIRS_PER_GROUP + g for g in range(CA_PAIRS_PER_GROUP)]
        heads = [(p, hh) for p in pairs for hh in range(per_pair)]
        lanes = {p: slice(p * LANES, (p + 1) * LANES) for p in pairs}
        qm = {(p, hh): jnp.where(in_head[hh], q_ref[:, lanes[p]].astype(F32), 0.0).astype(BF16)
              for p, hh in heads}
        sc = {(hd, jj): lax.dot_general(qm[hd], k_refs[jj][:, lanes[hd[0]]], (((1,), (1,)), ((), ())),
                                        preferred_element_type=F32)
              for hd in heads for jj in tiles}
        sc = {(hd, jj): jnp.where(i + jj >= CA_WINDOW - 1,
                                  sc[hd, jj] + bm_ref[hd[0] * per_pair + hd[1], :, jj * CA_BLOCK:(jj + 1) * CA_BLOCK],
                                  -1e30)
              for hd in heads for jj in tiles}
        m = {hd: functools.reduce(jnp.maximum, [jnp.max(sc[hd, jj], axis=-1, keepdims=True) for jj in tiles])
             for hd in heads}
        p_un = {(hd, jj): jnp.exp(sc[hd, jj] - m[hd]) for hd in heads for jj in tiles}
        denom = {hd: functools.reduce(add, [jnp.sum(p_un[hd, jj], axis=-1, keepdims=True) for jj in tiles])
                 for hd in heads}
        pv = {hd: functools.reduce(add, [jnp.dot(p_un[hd, jj].astype(BF16), v_refs[jj][:, lanes[hd[0]]],
                                                 preferred_element_type=F32) for jj in tiles])
              for hd in heads}
        for p in pairs:
            o_ref[:, lanes[p]] = functools.reduce(
                add, [jnp.where(in_head[hh], pv[p, hh] / denom[p, hh], 0.0) for hh in range(per_pair)]
            ).astype(o_ref.dtype)


def _ca_attention(q, k, v, bm, layer):
    s = q.shape[0]
    blk = lambda off: pl.BlockSpec((CA_BLOCK, CA_WIDTH), lambda i: (jnp.maximum(i - off, 0), 0))
    return pl.pallas_call(
        _ca_kernel,
        grid=(s // CA_BLOCK,),
        in_specs=[blk(0), blk(2), blk(1), blk(0), blk(2), blk(1), blk(0),
                  pl.BlockSpec((CA_HEADS,) + bm.shape[1:], lambda i: (layer, 0, 0))],
        out_specs=blk(0),
        out_shape=jax.ShapeDtypeStruct((s, CA_WIDTH), MIX_DTYPE),
        compiler_params=_cparams(("parallel",)),
        name="ca_attention",
    )(q, k, k, k, v, v, v, bm)


CA_BIAS_SPAN = 1024


def _ca_bias_kernel(g_ref, o_ref):
    width = CA_WINDOW * CA_BLOCK
    rows = jnp.broadcast_to(g_ref[0], (CA_BLOCK, CA_BIAS_SPAN))
    toeplitz = pltpu.roll(rows, 0, 1, stride=1, stride_axis=0)[:, :width]
    r = lax.broadcasted_iota(jnp.int32, (CA_BLOCK, width), 0)
    c = lax.broadcasted_iota(jnp.int32, (CA_BLOCK, width), 1)
    back = r // CHUNK - (c // CHUNK - LOOKBACK)
    o_ref[0] = jnp.where((back >= 0) & (back <= LOOKBACK), toeplitz, -1e30)


def _ca_bias_mask(rel_bias):
    planes = rel_bias.shape[0]
    width = CA_WINDOW * CA_BLOCK
    assert CA_BLOCK + width - 1 <= CA_BIAS_SPAN
    m = np.arange(CA_BIAS_SPAN)
    offset = np.where(m < width, m, m - CA_BIAS_SPAN)
    dist = LOOKBACK * CHUNK - offset
    by_offset = rel_bias.astype(F32)[:, np.clip(dist, -REL_CLIP, REL_CLIP) + REL_CLIP]
    return pl.pallas_call(
        _ca_bias_kernel,
        grid=(planes,),
        in_specs=[pl.BlockSpec((1, 1, CA_BIAS_SPAN), lambda i: (i, 0, 0))],
        out_specs=pl.BlockSpec((1, CA_BLOCK, width), lambda i: (i, 0, 0)),
        out_shape=jax.ShapeDtypeStruct((planes, CA_BLOCK, width), F32),
        compiler_params=_cparams(("parallel",)),
        name="ca_bias",
    )(by_offset.reshape(planes, 1, CA_BIAS_SPAN))


def _s5_disc_kernel(lre_ref, lim_ref, ldt_ref, bre_ref, bim_ref, are_o, aim_o, bbre_o, bbim_o):
    lre = lre_ref[...]
    lim = lim_ref[...]
    dt = jnp.exp(ldt_ref[...])
    mag = jnp.exp(lre * dt)
    ang = lim * dt
    a_re = mag * jnp.cos(ang)
    a_im = mag * jnp.sin(ang)
    are_o[...] = a_re
    aim_o[...] = a_im
    nre = a_re - 1.0
    nim = a_im
    den = lre * lre + lim * lim
    zoh_re = (nre * lre + nim * lim) / den
    zoh_im = (nim * lre - nre * lim) / den
    bre = bre_ref[...]
    bim = bim_ref[...]
    bbre_o[...] = zoh_re * bre - zoh_im * bim
    bbim_o[...] = zoh_re * bim + zoh_im * bre


def _s5_discretize(lam_re, lam_im, log_dt, b_re, b_im):
    col = lambda a: a.reshape(N_STATES, 1).astype(F32)
    ldt = jnp.repeat(log_dt.astype(F32), SSM_STATE).reshape(N_STATES, 1)
    b2 = lambda a: a.reshape(N_STATES, SSM_GROUP_CH).astype(F32)
    return pl.pallas_call(
        _s5_disc_kernel,
        out_shape=[jax.ShapeDtypeStruct((N_STATES, 1), F32)] * 2
        + [jax.ShapeDtypeStruct((N_STATES, SSM_GROUP_CH), F32)] * 2,
        name="s5_discretize",
    )(col(lam_re), col(lam_im), ldt, b2(b_re), b2(b_im))


S5_SEGS = SUBLANES
S5_STEPS = 64


def _cmul(a_re, a_im, b_re, b_im):
    return a_re * b_re - a_im * b_im, a_re * b_im + a_im * b_re


def _s5_local_kernel(u_ref, wre_ref, wim_ref, are_ref, aim_ref, cre_ref, cim_ref, y_o, end_o,
                     xre, xim, st):
    @pl.when(pl.program_id(0) == 0)
    def _():
        st[...] = jnp.zeros_like(st)

    u = u_ref[...].astype(BF16)
    xre[...] = jnp.dot(u, wre_ref[...], preferred_element_type=F32)
    xim[...] = jnp.dot(u, wim_ref[...], preferred_element_type=F32)
    a_re = jnp.broadcast_to(are_ref[...], (S5_SEGS, N_STATES))
    a_im = jnp.broadcast_to(aim_ref[...], (S5_SEGS, N_STATES))

    def step(k, c):
        rows = pl.ds(pl.multiple_of(k * S5_SEGS, S5_SEGS), S5_SEGS)
        p_re, p_im = _cmul(a_re, a_im, *c)
        n_re = p_re + xre[rows, :]
        n_im = p_im + xim[rows, :]
        xre[rows, :] = n_re
        xim[rows, :] = n_im
        return n_re, n_im

    x_re, x_im = lax.fori_loop(0, S5_STEPS, step, (st[0], st[1]), unroll=4)
    st[0] = x_re
    st[1] = x_im
    end_o[0] = x_re
    end_o[1] = x_im
    y_o[...] = (jnp.dot(xre[...].astype(BF16), cre_ref[...], preferred_element_type=F32)
                - jnp.dot(xim[...].astype(BF16), cim_ref[...], preferred_element_type=F32))


def _s5_finish_kernel(seg_len, y_ref, u_ref, end_ref, are_ref, aim_ref, cre_ref, cim_ref, d_ref, wglu_ref,
                      o_ref, zre, zim, x0, pw):
    a_re = jnp.broadcast_to(are_ref[...], (S5_SEGS, N_STATES))
    a_im = jnp.broadcast_to(aim_ref[...], (S5_SEGS, N_STATES))

    @pl.when(pl.program_id(0) == 0)
    def _():
        r_re, r_im = are_ref[...], aim_ref[...]
        p_re, p_im = jnp.ones_like(r_re), jnp.zeros_like(r_re)
        n = seg_len
        while n:
            if n & 1:
                p_re, p_im = _cmul(p_re, p_im, r_re, r_im)
            r_re, r_im = _cmul(r_re, r_im, r_re, r_im)
            n >>= 1
        rows_re, rows_im = [jnp.zeros_like(p_re)], [jnp.zeros_like(p_re)]
        for s in range(1, S5_SEGS):
            c_re, c_im = _cmul(p_re, p_im, rows_re[-1], rows_im[-1])
            rows_re.append(c_re + end_ref[0, s - 1:s, :])
            rows_im.append(c_im + end_ref[1, s - 1:s, :])
        x0[0] = jnp.concatenate(rows_re, axis=0)
        x0[1] = jnp.concatenate(rows_im, axis=0)
        pw[0] = a_re
        pw[1] = a_im

    x0_re = x0[0]
    x0_im = x0[1]

    def step(k, c):
        rows = pl.ds(pl.multiple_of(k * S5_SEGS, S5_SEGS), S5_SEGS)
        z_re, z_im = _cmul(c[0], c[1], x0_re, x0_im)
        zre[rows, :] = z_re
        zim[rows, :] = z_im
        return _cmul(a_re, a_im, *c)

    p_re, p_im = lax.fori_loop(0, S5_STEPS, step, (pw[0], pw[1]), unroll=4)
    pw[0] = p_re
    pw[1] = p_im
    y = y_ref[...] + (jnp.dot(zre[...].astype(BF16), cre_ref[...], preferred_element_type=F32)
                      - jnp.dot(zim[...].astype(BF16), cim_ref[...], preferred_element_type=F32))
    y = y + d_ref[...] * u_ref[...]
    y = 0.5 * y * (1.0 + jnp.tanh(math.sqrt(2.0 / math.pi) * (y + 0.044715 * (y * y * y))))
    hg = jnp.dot(y.astype(BF16), wglu_ref[...], preferred_element_type=F32)
    o_ref[...] = (hg[:, :SSM_WIDTH] / (1.0 + jnp.exp(-hg[:, SSM_WIDTH:]))).astype(o_ref.dtype)


def _s5_mixer(u, lam_re, lam_im, log_dt, b_re, b_im, c_re, c_im, d_skip, w_glu):
    s = u.shape[0]
    seg_len = s // S5_SEGS
    rows = S5_STEPS * S5_SEGS
    a_re, a_im, bb_re, bb_im = _s5_discretize(lam_re, lam_im, log_dt, b_re, b_im)
    a_re = a_re.reshape(1, N_STATES)
    a_im = a_im.reshape(1, N_STATES)
    eye = jnp.eye(SSM_GROUPS, dtype=F32)

    def in_matrix(bb):
        bb = bb.reshape(SSM_GROUPS, SSM_STATE, SSM_GROUP_CH)
        return jnp.einsum('gpc,gh->gchp', bb, eye).reshape(SSM_WIDTH, N_STATES).astype(BF16)

    def out_matrix(c):
        return jnp.einsum('gcp,gh->gphc', c.astype(F32), eye).reshape(N_STATES, SSM_WIDTH).astype(BF16)

    w_re, w_im = in_matrix(bb_re), in_matrix(bb_im)
    cm_re, cm_im = out_matrix(c_re), out_matrix(c_im)
    d = d_skip.reshape(1, SSM_WIDTH).astype(F32)
    wg = w_glu.astype(BF16)
    u_seg = u.reshape(S5_SEGS, seg_len, SSM_WIDTH).transpose(1, 0, 2).reshape(s, SSM_WIDTH)
    full = lambda a: pl.BlockSpec(a.shape, lambda i: (0,) * a.ndim)
    row = pl.BlockSpec((rows, SSM_WIDTH), lambda i: (i, 0))
    ends = pl.BlockSpec((2, S5_SEGS, N_STATES), lambda i: (0, 0, 0))
    big = pltpu.VMEM((rows, N_STATES), F32)
    small = pltpu.VMEM((2, S5_SEGS, N_STATES), F32)
    y_local, end = pl.pallas_call(
        _s5_local_kernel,
        grid=(seg_len // S5_STEPS,),
        in_specs=[row] + [full(a) for a in (w_re, w_im, a_re, a_im, cm_re, cm_im)],
        out_specs=[row, ends],
        out_shape=[jax.ShapeDtypeStruct((s, SSM_WIDTH), F32),
                   jax.ShapeDtypeStruct((2, S5_SEGS, N_STATES), F32)],
        scratch_shapes=[big, big, small],
        compiler_params=_cparams(("arbitrary",)),
        name="s5_local",
    )(u_seg, w_re, w_im, a_re, a_im, cm_re, cm_im)
    o_seg = pl.pallas_call(
        functools.partial(_s5_finish_kernel, seg_len),
        grid=(seg_len // S5_STEPS,),
        in_specs=[row, row, ends] + [full(a) for a in (a_re, a_im, cm_re, cm_im, d, wg)],
        out_specs=row,
        out_shape=jax.ShapeDtypeStruct((s, SSM_WIDTH), MIX_DTYPE),
        scratch_shapes=[big, big, small, small],
        compiler_params=_cparams(("arbitrary",)),
        name="s5_finish",
    )(y_local, u_seg, end, a_re, a_im, cm_re, cm_im, d, wg)
    return o_seg.reshape(seg_len, S5_SEGS, SSM_WIDTH).transpose(1, 0, 2).reshape(s, SSM_WIDTH)


def _mix_core(osb_ref, oca_ref, ossm_ref, x_ref, g_ref, w_ref, nf_ref):
    g = g_ref[...]
    mixed = jnp.concatenate([
        _rms(osb_ref[...].astype(F32), g[:, :SB_WIDTH]),
        _rms(oca_ref[...].astype(F32), g[:, SB_WIDTH:SB_WIDTH + CA_WIDTH]),
        _rms(ossm_ref[...].astype(F32), g[:, SB_WIDTH + CA_WIDTH:]),
    ], axis=-1)
    x1 = x_ref[...] + jnp.dot(mixed.astype(BF16), w_ref[...], preferred_element_type=F32)
    return x1, _rms(x1, nf_ref[...])


def _mix_dense_kernel(osb_ref, oca_ref, ossm_ref, x_ref, g_ref, w_ref, nf_ref, x1_o, h2_o):
    x1, h2 = _mix_core(osb_ref, oca_ref, ossm_ref, x_ref, g_ref, w_ref, nf_ref)
    x1_o[...] = x1
    h2_o[...] = h2.astype(BF16)


def _mix_moe_kernel(osb_ref, oca_ref, ossm_ref, x_ref, g_ref, w_ref, nf_ref, rboth_ref, rhi_ref,
                    x1_o, h2_o, route_o):
    x1, h2 = _mix_core(osb_ref, oca_ref, ossm_ref, x_ref, g_ref, w_ref, nf_ref)
    x1_o[...] = x1
    h2_o[...] = h2
    hi, lo = _split_bf16(h2)
    nt = (((1,), (1,)), ((), ()))
    both = lax.dot_general(rboth_ref[...], hi, nt, preferred_element_type=F32)
    cross = lax.dot_general(rhi_ref[...], lo, nt, preferred_element_type=F32)
    lg = both[:N_EXPERTS] + both[N_EXPERTS:] + cross[:N_EXPERTS]
    idx = lax.broadcasted_iota(jnp.int32, lg.shape, 0)
    neg = -jnp.inf
    m1 = jnp.max(lg, axis=0, keepdims=True)
    i1 = jnp.min(jnp.where(lg == m1, idx, N_EXPERTS), axis=0, keepdims=True)
    lg2 = jnp.where(idx == i1, neg, lg)
    m2 = jnp.max(lg2, axis=0, keepdims=True)
    i2 = jnp.min(jnp.where(lg2 == m2, idx, N_EXPERTS), axis=0, keepdims=True)
    e = jnp.exp(m2 - m1)
    g1 = 1.0 / (1.0 + e)
    g2 = e / (1.0 + e)
    route_o[...] = jnp.where(idx == 0, i1.astype(F32),
                             jnp.where(idx == 1, i2.astype(F32),
                                       jnp.where(idx == 2, g1, jnp.where(idx == 3, g2, 0.0))))


def _mix_out(o_sb, o_ca, o_ssm, x, g, w, nf, router=None):
    s = x.shape[0]
    row = lambda w_: pl.BlockSpec((TM, w_), lambda i: (i, 0))
    full = lambda a: pl.BlockSpec(a.shape, lambda i: (0, 0))
    ins = [o_sb, o_ca, o_ssm, x, g, w, nf]
    in_specs = [row(SB_WIDTH), row(CA_WIDTH), row(SSM_WIDTH), row(D_MODEL), full(g), full(w), full(nf)]
    if router is None:
        return pl.pallas_call(
            _mix_dense_kernel, grid=(s // TM,), in_specs=in_specs,
            out_specs=[row(D_MODEL)] * 2,
            out_shape=[jax.ShapeDtypeStruct((s, D_MODEL), F32), jax.ShapeDtypeStruct((s, D_MODEL), BF16)],
            compiler_params=_cparams(("parallel",)), name="mix_out_dense",
        )(*ins)
    return pl.pallas_call(
        _mix_moe_kernel, grid=(s // TM,), in_specs=in_specs + [full(router[0]), full(router[1])],
        out_specs=[row(D_MODEL), row(D_MODEL), pl.BlockSpec((N_EXPERTS, TM), lambda i: (0, i))],
        out_shape=[jax.ShapeDtypeStruct((s, D_MODEL), F32)] * 2 + [jax.ShapeDtypeStruct((N_EXPERTS, s), F32)],
        compiler_params=_cparams(("parallel",)), name="mix_out_moe",
    )(*ins, *router)


def _swiglu_tile(xb, wg, wu, wd):
    hg = jnp.dot(xb, wg, preferred_element_type=F32)
    hu = jnp.dot(xb, wu, preferred_element_type=F32)
    hid = hg / (1.0 + jnp.exp(-hg)) * hu
    return jnp.dot(hid.astype(BF16), wd, preferred_element_type=F32)


FF_CHUNKS = (1024, 1024, 768)


def _swiglu_chunks(xb, wg_ref, wu_ref, wd_ref):
    assert sum(FF_CHUNKS) == D_FF
    y = None
    start = 0
    for width in FF_CHUNKS:
        cols = slice(start, start + width)
        part = _swiglu_tile(xb, wg_ref[:, cols], wu_ref[:, cols], wd_ref[cols, :])
        y = part if y is None else y + part
        start += width
    return y


def _ffn_kernel(h_ref, x_ref, wg_ref, wu_ref, wd_ref, o_ref):
    o_ref[...] = x_ref[...] + _swiglu_chunks(h_ref[...], wg_ref.at[0], wu_ref.at[0], wd_ref.at[0])


def _ffn_dense(h2, x1, wg, wu, wd, layer):
    s = x1.shape[0]
    row = pl.BlockSpec((TM, D_MODEL), lambda i: (i, 0))
    resident = lambda a: pl.BlockSpec((1,) + a.shape[1:], lambda i: (layer, 0, 0), pipeline_mode=pl.Buffered(1))
    return pl.pallas_call(
        _ffn_kernel,
        grid=(s // TM,),
        in_specs=[row, row, resident(wg), resident(wu), resident(wd)],
        out_specs=row,
        out_shape=jax.ShapeDtypeStruct((s, D_MODEL), F32),
        compiler_params=_cparams(("parallel",)),
        name="ffn_dense",
    )(h2, x1, wg, wu, wd)


def _for_rows(n, fn):
    for r in range(n):
        fn(r)


def _dispatch_kernel(fill_blk, n_used, pos_ref, h_ref, xs_hbm, zbuf, sem, zsem):
    n_blocks = xs_hbm.shape[0] // MOE_BLOCK

    def zero_fill(blk):
        return pltpu.make_async_copy(
            zbuf, xs_hbm.at[pl.ds(pl.multiple_of(blk * MOE_BLOCK, MOE_BLOCK), MOE_BLOCK)], zsem)

    @pl.when(pl.program_id(0) == 0)
    def _():
        zbuf[...] = jnp.zeros_like(zbuf)
        def fill(blk):
            cp = zero_fill(blk)
            cp.start()
            cp.wait()

        for e in range(N_EXPERTS):
            fill(fill_blk[e])

        def tail(blk, c):
            fill(blk)
            return c
        lax.fori_loop(n_used[0], n_blocks, tail, 0)

    def row_copy(r, k):
        return pltpu.make_async_copy(h_ref.at[pl.ds(r, 1)], xs_hbm.at[pl.ds(pos_ref[0, 0, TOP_K * r + k], 1)], sem)

    def start(r):
        for k in range(TOP_K):
            row_copy(r, k).start(priority=k)

    def wait(r):
        for k in range(TOP_K):
            row_copy(r, k).wait()

    _for_rows(TM, start)
    _for_rows(TM, wait)


def _moe_dispatch(h2, pos, fill_blk, n_used, n_blocks):
    s = h2.shape[0]
    grid_spec = pltpu.PrefetchScalarGridSpec(
        num_scalar_prefetch=2,
        grid=(s // TM,),
        in_specs=[pl.BlockSpec((1, 1, TOP_K * TM), lambda i, f, u: (i, 0, 0), memory_space=pltpu.SMEM),
                  pl.BlockSpec((TM, D_MODEL), lambda i, f, u: (i, 0))],
        out_specs=pl.BlockSpec(memory_space=pl.ANY),
        scratch_shapes=[pltpu.VMEM((MOE_BLOCK, D_MODEL), F32), pltpu.SemaphoreType.DMA,
                        pltpu.SemaphoreType.DMA],
    )
    return pl.pallas_call(
        _dispatch_kernel,
        grid_spec=grid_spec,
        out_shape=jax.ShapeDtypeStruct((n_blocks * MOE_BLOCK, D_MODEL), F32),
        compiler_params=_cparams(("arbitrary",)),
        name="moe_dispatch",
    )(fill_blk, n_used, pos.reshape(s // TM, 1, TOP_K * TM), h2)


def _moe_kernel(blk_e, n_used, xs_ref, wg_ref, wu_ref, wd_ref, o_ref):
    b = pl.program_id(0)

    @pl.when(b < n_used[0])
    def _():
        o_ref[...] = _swiglu_chunks(xs_ref[...].astype(BF16), wg_ref.at[0, 0], wu_ref.at[0, 0], wd_ref.at[0, 0])

    @pl.when(b >= n_used[0])
    def _():
        o_ref[...] = jnp.zeros_like(o_ref)


def _moe_experts(xs, blk_e, n_used, wg, wu, wd, layer, n_blocks):
    weights = lambda a: pl.BlockSpec((1, 1) + a.shape[2:], lambda b, e, u: (layer, e[b], 0, 0))
    grid_spec = pltpu.PrefetchScalarGridSpec(
        num_scalar_prefetch=2,
        grid=(n_blocks,),
        in_specs=[pl.BlockSpec((MOE_BLOCK, D_MODEL), lambda b, e, u: (jnp.minimum(b, u[0] - 1), 0)),
                  weights(wg), weights(wu), weights(wd)],
        out_specs=pl.BlockSpec((MOE_BLOCK, D_MODEL), lambda b, e, u: (b, 0)),
    )
    return pl.pallas_call(
        _moe_kernel,
        grid_spec=grid_spec,
        out_shape=jax.ShapeDtypeStruct((n_blocks * MOE_BLOCK, D_MODEL), F32),
        compiler_params=_cparams(("arbitrary",)),
        name="moe_experts",
    )(blk_e, n_used, xs, wg, wu, wd)


def _combine_kernel(pos_ref, x_ref, gate_ref, ys_hbm, o_ref, ybuf, sem):
    def row_copy(r, k):
        return pltpu.make_async_copy(ys_hbm.at[pl.ds(pos_ref[0, 0, TOP_K * r + k], 1)],
                                     ybuf.at[k, pl.ds(r, 1)], sem)

    def start(r):
        for k in range(TOP_K):
            row_copy(r, k).start(priority=k)

    def wait(r):
        for k in range(TOP_K):
            row_copy(r, k).wait()

    _for_rows(TM, start)
    _for_rows(TM, wait)
    gates = gate_ref[...]
    gated = [ybuf[k] * gates[:, k:k + 1] for k in range(TOP_K)]
    o_ref[...] = x_ref[...] + functools.reduce(lambda a, b_: a + b_, gated)


def _moe_combine(x1, gates, pos, ys):
    s = x1.shape[0]
    row = lambda w_: pl.BlockSpec((TM, w_), lambda i: (i, 0))
    return pl.pallas_call(
        _combine_kernel,
        grid=(s // TM,),
        in_specs=[pl.BlockSpec((1, 1, TOP_K * TM), lambda i: (i, 0, 0), memory_space=pltpu.SMEM),
                  row(D_MODEL), row(TOP_K), pl.BlockSpec(memory_space=pl.ANY)],
        out_specs=row(D_MODEL),
        out_shape=jax.ShapeDtypeStruct((s, D_MODEL), F32),
        scratch_shapes=[pltpu.VMEM((TOP_K, TM, D_MODEL), F32), pltpu.SemaphoreType.DMA],
        compiler_params=_cparams(("arbitrary",)),
        name="moe_combine",
    )(pos.reshape(s // TM, 1, TOP_K * TM), x1, gates, ys)


def _moe_layer(x1, h2, route, wg, wu, wd, layer):
    s = h2.shape[0]
    n_blocks = s * TOP_K // MOE_BLOCK + N_EXPERTS
    flat_e = route[:TOP_K].T.astype(jnp.int32).reshape(-1)
    gates = route[TOP_K:2 * TOP_K].T
    onehot = (flat_e[:, None] == jnp.arange(N_EXPERTS)[None, :]).astype(jnp.int32)
    csum = jnp.cumsum(onehot, axis=0)
    counts = csum[-1]
    padded = (counts + MOE_BLOCK - 1) // MOE_BLOCK * MOE_BLOCK
    pend = jnp.cumsum(padded)
    pstart = pend - padded
    pos = jnp.sum(onehot * (csum - 1 + pstart[None, :]), axis=1).astype(jnp.int32)
    blk_start = jnp.arange(n_blocks, dtype=jnp.int32) * MOE_BLOCK
    blk_e = jnp.minimum(jnp.sum((pend[None, :] <= blk_start[:, None]).astype(jnp.int32), axis=1),
                        N_EXPERTS - 1).astype(jnp.int32)
    n_used = (pend[-1] // MOE_BLOCK).astype(jnp.int32).reshape(1)
    fill_blk = jnp.maximum(pend // MOE_BLOCK - 1, 0).astype(jnp.int32)
    xs = _moe_dispatch(h2, pos, fill_blk, n_used, n_blocks)
    ys = _moe_experts(xs, blk_e, n_used, wg, wu, wd, layer, n_blocks)
    return _moe_combine(x1, gates, pos, ys)


def _block_diag_ones(width, group):
    idx = jnp.arange(width) // group
    return (idx[:, None] == idx[None, :]).astype(BF16)


def kernel(x, norm_mix, w_in, q_norm_sb, k_norm_sb, q_norm_ca, k_norm_ca, rel_bias, lam_re, lam_im, log_dt, b_re, b_im, c_re, c_im, d_skip, w_glu, g_out, w_out, norm_ffn, w_gate_dense, w_up_dense, w_down_dense, w_router, w_gate_moe, w_up_moe, w_down_moe):
    b, s, d = x.shape
    assert b == 1 and d == D_MODEL and s % (TM * 2) == 0
    xs = x.reshape(s, d).astype(F32)
    scale = 1.0 / math.sqrt(HEAD_DIM)
    gm = _block_diag_ones(GROUP_SUM_WIDTH, HEAD_DIM)
    idx = jnp.arange(SB_BLOCK)
    tri = (idx[:, None] > idx[None, :]).astype(BF16)
    vec = lambda a: a.reshape(1, -1).astype(F32)
    tile = lambda a, width: jnp.tile(a.astype(F32), width // HEAD_DIM).reshape(1, width)

    dense_w = [w.astype(BF16) for w in (w_gate_dense, w_up_dense, w_down_dense)]
    moe_w = [w.astype(BF16) for w in (w_gate_moe, w_up_moe, w_down_moe)]
    ca_bias = _ca_bias_mask(rel_bias.reshape(DEPTH * CA_HEADS, 2 * REL_CLIP + 1))

    for l in range(DEPTH):
        q_sb, k_sb, v_sb, q_ca, k_ca, v_ca, u = _in_proj(
            xs, vec(norm_mix[l]), w_in[l].astype(BF16),
            tile(q_norm_sb[l], SB_WIDTH) * (scale * LOG2E), tile(k_norm_sb[l], SB_WIDTH),
            tile(q_norm_ca[l], CA_WIDTH) * scale, tile(k_norm_ca[l], CA_WIDTH), gm)
        o_sb = _sb_attention(q_sb, k_sb, v_sb, tri)
        o_ca = _ca_attention(q_ca, k_ca, v_ca, ca_bias, layer=l)
        o_ssm = _s5_mixer(u, lam_re[l], lam_im[l], log_dt[l], b_re[l], b_im[l], c_re[l], c_im[l],
                          d_skip[l], w_glu[l])
        i = l // 2
        if l % 2 == 0:
            x1, h2 = _mix_out(o_sb, o_ca, o_ssm, xs, vec(g_out[l]), w_out[l].astype(BF16), vec(norm_ffn[l]))
            xs = _ffn_dense(h2, x1, *dense_w, layer=i)
        else:
            r_hi, r_lo = _split_bf16(w_router[i].astype(F32).T)
            r_both = jnp.concatenate([r_hi, r_lo], axis=0)
            r_hi = jnp.concatenate([r_hi, jnp.zeros_like(r_hi)], axis=0)
            x1, h2, route = _mix_out(o_sb, o_ca, o_ssm, xs, vec(g_out[l]), w_out[l].astype(BF16),
                                     vec(norm_ffn[l]), router=(r_both, r_hi))
            xs = _moe_layer(x1, h2, route, *moe_w, layer=i)
    return xs.reshape(b, s, d).astype(x.dtype)
```

```python
import functools
import math

import jax
import jax.numpy as jnp
import numpy as np
from jax import lax
from jax.experimental import pallas as pl
from jax.experimental.pallas import tpu as pltpu

F32 = jnp.float32
BF16 = jnp.bfloat16
MIX_DTYPE = BF16

D_MODEL = 1024
DEPTH = 4
CHUNK = 64
HEAD_DIM = 64
SB_WIDTH = 256
CA_WIDTH = 512
CA_HEADS = 8
SSM_WIDTH = 256
SSM_GROUP_CH = 16
SSM_GROUPS = 16
SSM_STATE = 64
N_STATES = SSM_GROUPS * SSM_STATE
D_IN_PROJ = 3 * SB_WIDTH + 3 * CA_WIDTH + SSM_WIDTH
MASK_LOGIT = -1e4
LOOKBACK = 8
REL_CLIP = 128
D_FF = 2816
N_EXPERTS = 8
TOP_K = 2
RMS_EPS = 1e-6

LANES = 128
SUBLANES = 8
VMEM_LIMIT = 56 * 1024 * 1024

TM = 512
SB_BLOCK = 256
SB_BLOCKS_PER_ITER = 2
CA_BLOCK = 256
MOE_BLOCK = 512
SB_UNDERFLOW = 104.0
LOG2E = math.log2(math.e)


def _cparams(sem):
    return pltpu.CompilerParams(dimension_semantics=sem, vmem_limit_bytes=VMEM_LIMIT)


def _split_bf16(x):
    hi = x.astype(BF16)
    lo = (x - hi.astype(F32)).astype(BF16)
    return hi, lo


def _rms(x, gain):
    ms = jnp.mean(x * x, axis=-1, keepdims=True)
    return x * lax.rsqrt(ms + RMS_EPS) * gain


GROUP_SUM_WIDTH = 256


def _head_rms(p, gmat, gain):
    sq = (p * p).astype(BF16)
    blocks = []
    for c in range(p.shape[1] // GROUP_SUM_WIDTH):
        cols = slice(c * GROUP_SUM_WIDTH, (c + 1) * GROUP_SUM_WIDTH)
        blocks.append(jnp.dot(sq[:, cols], gmat, preferred_element_type=F32))
    ss = blocks[0] if len(blocks) == 1 else jnp.concatenate(blocks, axis=1)
    return p * lax.rsqrt(ss * (1.0 / HEAD_DIM) + RMS_EPS) * gain


def _in_proj_kernel(x_ref, g_ref, w_ref, gq_sb, gk_sb, gq_ca, gk_ca, gm_ref,
                    qsb_o, ksb_o, vsb_o, qca_o, kca_o, vca_o, u_o):
    h = _rms(x_ref[...], g_ref[...])
    proj = jnp.dot(h.astype(BF16), w_ref[...], preferred_element_type=F32)
    gm = gm_ref[...]
    o1, o2, o3 = SB_WIDTH, 2 * SB_WIDTH, 3 * SB_WIDTH
    o4, o5, o6 = o3 + CA_WIDTH, o3 + 2 * CA_WIDTH, o3 + 3 * CA_WIDTH
    qsb_o[...] = _head_rms(proj[:, :o1], gm, gq_sb[...]).astype(BF16)
    ksb_o[...] = _head_rms(proj[:, o1:o2], gm, gk_sb[...]).astype(BF16)
    vsb_o[...] = proj[:, o2:o3].astype(BF16)
    qca_o[...] = _head_rms(proj[:, o3:o4], gm, gq_ca[...]).astype(BF16)
    kca_o[...] = _head_rms(proj[:, o4:o5], gm, gk_ca[...]).astype(BF16)
    vca_o[...] = proj[:, o5:o6].astype(BF16)
    u_o[...] = proj[:, o6:]


def _in_proj(x, g, w, gq_sb, gk_sb, gq_ca, gk_ca, gm):
    s = x.shape[0]
    row = lambda w_: pl.BlockSpec((TM, w_), lambda i: (i, 0))
    full = lambda a: pl.BlockSpec(a.shape, lambda i: (0,) * a.ndim)
    return pl.pallas_call(
        _in_proj_kernel,
        grid=(s // TM,),
        in_specs=[row(D_MODEL)] + [full(a) for a in (g, w, gq_sb, gk_sb, gq_ca, gk_ca, gm)],
        out_specs=[row(SB_WIDTH)] * 3 + [row(CA_WIDTH)] * 3 + [row(SSM_WIDTH)],
        out_shape=[jax.ShapeDtypeStruct((s, SB_WIDTH), BF16)] * 3
        + [jax.ShapeDtypeStruct((s, CA_WIDTH), BF16)] * 3
        + [jax.ShapeDtypeStruct((s, SSM_WIDTH), F32)],
        compiler_params=_cparams(("parallel",)),
        name="in_proj",
    )(x, g, w, gq_sb, gk_sb, gq_ca, gk_ca, gm)


def _sb_kernel(q_ref, k_ref, v_ref, tri_ref, o_ref):
    i = pl.program_id(0)
    tri = tri_ref[...]
    row = lax.broadcasted_iota(jnp.int32, (SB_BLOCK, SB_BLOCK), 0)
    col = lax.broadcasted_iota(jnp.int32, (SB_BLOCK, SB_BLOCK), 1)
    rel = col - row
    lane = lax.broadcasted_iota(jnp.int32, (SB_BLOCK, LANES), 1)
    n_pairs = SB_WIDTH // LANES
    per_pair = LANES // HEAD_DIM
    in_head = [(lane >= hh * HEAD_DIM) & (lane < (hh + 1) * HEAD_DIM) for hh in range(per_pair)]
    qm = [jnp.where(in_head[hh], q_ref[:, p * LANES:(p + 1) * LANES].astype(F32), 0.0).astype(BF16)
          for p in range(n_pairs) for hh in range(per_pair)]

    def cond(c):
        j, done = c[0], c[1]
        return jnp.logical_and(j >= 0, jnp.logical_not(done))

    def body(c):
        j, _, carries, accs = c
        heads = range(n_pairs * per_pair)
        lanes = [slice(p * LANES, (p + 1) * LANES) for p in range(n_pairs)]
        add = lambda a, b_: a + b_
        blocks = range(SB_BLOCKS_PER_ITER)
        starts = [pl.multiple_of(jnp.maximum(j - b_, 0) * SB_BLOCK, SB_BLOCK) for b_ in blocks]
        visible = [jnp.logical_and(rel < (i - j + b_) * SB_BLOCK, j - b_ >= 0) for b_ in blocks]
        kb = [[k_ref[pl.ds(starts[b_], SB_BLOCK), lanes[p]] for p in range(n_pairs)] for b_ in blocks]
        vb = [[v_ref[pl.ds(starts[b_], SB_BLOCK), lanes[p]] for p in range(n_pairs)] for b_ in blocks]
        tiles = [(b_, h) for b_ in blocks for h in heads]
        z = {t: lax.dot_general(qm[t[1]], kb[t[0]][t[1] // per_pair], (((1,), (1,)), ((), ())),
                                preferred_element_type=F32) for t in tiles}
        z = {t: jnp.where(visible[t[0]], z[t], MASK_LOGIT * LOG2E) for t in tiles}
        sp = {t: jnp.maximum(z[t], 0.0) + jnp.log2(1.0 + jnp.exp2(-jnp.abs(z[t]))) for t in tiles}
        later = {t: jnp.dot(sp[t].astype(BF16), tri, preferred_element_type=F32) for t in tiles}
        total = {t: jnp.sum(sp[t], axis=-1, keepdims=True) for t in tiles}
        carry = {}
        for h in heads:
            running = carries[h]
            for b_ in blocks:
                carry[b_, h] = running
                running = running + total[b_, h]
            carry[SB_BLOCKS_PER_ITER, h] = running
        w = {t: jnp.exp2(z[t] - sp[t] - (later[t] + carry[t])).astype(BF16) for t in tiles}
        pv = {t: jnp.dot(w[t], vb[t[0]][t[1] // per_pair], preferred_element_type=F32) for t in tiles}
        new_accs = [accs[p] + functools.reduce(
            add, [jnp.where(in_head[hh], functools.reduce(add, [pv[b_, p * per_pair + hh] for b_ in blocks]), 0.0)
                  for hh in range(per_pair)])
            for p in range(n_pairs)]
        new_carries = [carry[SB_BLOCKS_PER_ITER, h] for h in heads]
        done = jnp.min(functools.reduce(jnp.minimum, new_carries)) > SB_UNDERFLOW * LOG2E
        return j - SB_BLOCKS_PER_ITER, done, tuple(new_carries), tuple(new_accs)

    init = (i, False,
            tuple(jnp.zeros((SB_BLOCK, 1), F32) for _ in range(n_pairs * per_pair)),
            tuple(jnp.zeros((SB_BLOCK, LANES), F32) for _ in range(n_pairs)))
    accs = lax.while_loop(cond, body, init)[3]
    for p in range(n_pairs):
        o_ref[:, p * LANES:(p + 1) * LANES] = accs[p].astype(o_ref.dtype)


def _sb_attention(q, k, v, tri):
    s = q.shape[0]
    whole = pl.BlockSpec(memory_space=pltpu.VMEM)
    return pl.pallas_call(
        _sb_kernel,
        grid=(s // SB_BLOCK,),
        in_specs=[pl.BlockSpec((SB_BLOCK, SB_WIDTH), lambda i: (i, 0)), whole, whole, whole],
        out_specs=pl.BlockSpec((SB_BLOCK, SB_WIDTH), lambda i: (i, 0)),
        out_shape=jax.ShapeDtypeStruct((s, SB_WIDTH), MIX_DTYPE),
        compiler_params=_cparams(("parallel",)),
        name="sb_attention",
    )(q, k, v, tri)


CA_WINDOW = 3
CA_PAIRS_PER_GROUP = 2


def _ca_kernel(q_ref, k0, k1, k2, v0, v1, v2, bm_ref, o_ref):
    i = pl.program_id(0)
    k_refs = (k0, k1, k2)
    v_refs = (v0, v1, v2)
    lane = lax.broadcasted_iota(jnp.int32, (CA_BLOCK, LANES), 1)
    per_pair = LANES // HEAD_DIM
    in_head = [(lane >= hh * HEAD_DIM) & (lane < (hh + 1) * HEAD_DIM) for hh in range(per_pair)]
    add = lambda a, b_: a + b_
    tiles = range(CA_WINDOW)
    for group in range(CA_WIDTH // LANES // CA_PAIRS_PER_GROUP):
        pairs = [group * CA_PAIRS_PER_GROUP + g for g in range(CA_PAIRS_PER_GROUP)]
        heads = [(p, hh) for p in pairs for hh in range(per_pair)]
        lanes = {p: slice(p * LANES, (p + 1) * LANES) for p in pairs}
        qm = {(p, hh): jnp.where(in_head[hh], q_ref[:, lanes[p]].astype(F32), 0.0).astype(BF16)
              for p, hh in heads}
        sc = {(hd, jj): lax.dot_general(qm[hd], k_refs[jj][:, lanes[hd[0]]], (((1,), (1,)), ((), ())),
                                        preferred_element_type=F32)
              for hd in heads for jj in tiles}
        sc = {(hd, jj): jnp.where(i + jj >= CA_WINDOW - 1,
                                  sc[hd, jj] + bm_ref[hd[0] * per_pair + hd[1], :, jj * CA_BLOCK:(jj + 1) * CA_BLOCK],
                                  -1e30)
              for hd in heads for jj in tiles}
        m = {hd: functools.reduce(jnp.maximum, [jnp.max(sc[hd, jj], axis=-1, keepdims=True) for jj in tiles])
             for hd in heads}
        p_un = {(hd, jj): jnp.exp(sc[hd, jj] - m[hd]) for hd in heads for jj in tiles}
        denom = {hd: functools.reduce(add, [jnp.sum(p_un[hd, jj], axis=-1, keepdims=True) for jj in tiles])
                 for hd in heads}
        pv = {hd: functools.reduce(add, [jnp.dot(p_un[hd, jj].astype(BF16), v_refs[jj][:, lanes[hd[0]]],
                                                 preferred_element_type=F32) for jj in tiles])
              for hd in heads}
        for p in pairs:
            o_ref[:, lanes[p]] = functools.reduce(
                add, [jnp.where(in_head[hh], pv[p, hh] / denom[p, hh], 0.0) for hh in range(per_pair)]
            ).astype(o_ref.dtype)


def _ca_attention(q, k, v, bm, layer):
    s = q.shape[0]
    blk = lambda off: pl.BlockSpec((CA_BLOCK, CA_WIDTH), lambda i: (jnp.maximum(i - off, 0), 0))
    return pl.pallas_call(
        _ca_kernel,
        grid=(s // CA_BLOCK,),
        in_specs=[blk(0), blk(2), blk(1), blk(0), blk(2), blk(1), blk(0),
                  pl.BlockSpec((CA_HEADS,) + bm.shape[1:], lambda i: (layer, 0, 0))],
        out_specs=blk(0),
        out_shape=jax.ShapeDtypeStruct((s, CA_WIDTH), MIX_DTYPE),
        compiler_params=_cparams(("parallel",)),
        name="ca_attention",
    )(q, k, k, k, v, v, v, bm)


CA_BIAS_SPAN = 1024


def _ca_bias_kernel(g_ref, o_ref):
    width = CA_WINDOW * CA_BLOCK
    rows = jnp.broadcast_to(g_ref[0], (CA_BLOCK, CA_BIAS_SPAN))
    toeplitz = pltpu.roll(rows, 0, 1, stride=1, stride_axis=0)[:, :width]
    r = lax.broadcasted_iota(jnp.int32, (CA_BLOCK, width), 0)
    c = lax.broadcasted_iota(jnp.int32, (CA_BLOCK, width), 1)
    back = r // CHUNK - (c // CHUNK - LOOKBACK)
    o_ref[0] = jnp.where((back >= 0) & (back <= LOOKBACK), toeplitz, -1e30)


def _ca_bias_mask(rel_bias):
    planes = rel_bias.shape[0]
    width = CA_WINDOW * CA_BLOCK
    assert CA_BLOCK + width - 1 <= CA_BIAS_SPAN
    m = np.arange(CA_BIAS_SPAN)
    offset = np.where(m < width, m, m - CA_BIAS_SPAN)
    dist = LOOKBACK * CHUNK - offset
    by_offset = rel_bias.astype(F32)[:, np.clip(dist, -REL_CLIP, REL_CLIP) + REL_CLIP]
    return pl.pallas_call(
        _ca_bias_kernel,
        grid=(planes,),
        in_specs=[pl.BlockSpec((1, 1, CA_BIAS_SPAN), lambda i: (i, 0, 0))],
        out_specs=pl.BlockSpec((1, CA_BLOCK, width), lambda i: (i, 0, 0)),
        out_shape=jax.ShapeDtypeStruct((planes, CA_BLOCK, width), F32),
        compiler_params=_cparams(("parallel",)),
        name="ca_bias",
    )(by_offset.reshape(planes, 1, CA_BIAS_SPAN))


def _s5_disc_kernel(lre_ref, lim_ref, ldt_ref, bre_ref, bim_ref, are_o, aim_o, bbre_o, bbim_o):
    lre = lre_ref[...]
    lim = lim_ref[...]
    dt = jnp.exp(ldt_ref[...])
    mag = jnp.exp(lre * dt)
    ang = lim * dt
    a_re = mag * jnp.cos(ang)
    a_im = mag * jnp.sin(ang)
    are_o[...] = a_re
    aim_o[...] = a_im
    nre = a_re - 1.0
    nim = a_im
    den = lre * lre + lim * lim
    zoh_re = (nre * lre + nim * lim) / den
    zoh_im = (nim * lre - nre * lim) / den
    bre = bre_ref[...]
    bim = bim_ref[...]
    bbre_o[...] = zoh_re * bre - zoh_im * bim
    bbim_o[...] = zoh_re * bim + zoh_im * bre


def _s5_discretize(lam_re, lam_im, log_dt, b_re, b_im):
    col = lambda a: a.reshape(N_STATES, 1).astype(F32)
    ldt = jnp.repeat(log_dt.astype(F32), SSM_STATE).reshape(N_STATES, 1)
    b2 = lambda a: a.reshape(N_STATES, SSM_GROUP_CH).astype(F32)
    return pl.pallas_call(
        _s5_disc_kernel,
        out_shape=[jax.ShapeDtypeStruct((N_STATES, 1), F32)] * 2
        + [jax.ShapeDtypeStruct((N_STATES, SSM_GROUP_CH), F32)] * 2,
        name="s5_discretize",
    )(col(lam_re), col(lam_im), ldt, b2(b_re), b2(b_im))


S5_SEGS = SUBLANES
S5_STEPS = 128


def _cmul(a_re, a_im, b_re, b_im):
    return a_re * b_re - a_im * b_im, a_re * b_im + a_im * b_re


def _s5_local_kernel(u_ref, wre_ref, wim_ref, are_ref, aim_ref, cre_ref, cim_ref, y_o, end_o,
                     xre, xim, st):
    @pl.when(pl.program_id(0) == 0)
    def _():
        st[...] = jnp.zeros_like(st)

    u = u_ref[...].astype(BF16)
    xre[...] = jnp.dot(u, wre_ref[...], preferred_element_type=F32)
    xim[...] = jnp.dot(u, wim_ref[...], preferred_element_type=F32)
    a_re = jnp.broadcast_to(are_ref[...], (S5_SEGS, N_STATES))
    a_im = jnp.broadcast_to(aim_ref[...], (S5_SEGS, N_STATES))
    x_re, x_im = st[0], st[1]
    for k in range(S5_STEPS):
        rows = slice(k * S5_SEGS, (k + 1) * S5_SEGS)
        p_re, p_im = _cmul(a_re, a_im, x_re, x_im)
        x_re = p_re + xre[rows, :]
        x_im = p_im + xim[rows, :]
        xre[rows, :] = x_re
        xim[rows, :] = x_im
    st[0] = x_re
    st[1] = x_im
    end_o[0] = x_re
    end_o[1] = x_im
    y_o[...] = (jnp.dot(xre[...].astype(BF16), cre_ref[...], preferred_element_type=F32)
                - jnp.dot(xim[...].astype(BF16), cim_ref[...], preferred_element_type=F32))


def _s5_finish_kernel(seg_len, y_ref, u_ref, end_ref, are_ref, aim_ref, cre_ref, cim_ref, d_ref, wglu_ref,
                      o_ref, zre, zim, x0, pw):
    a_re = jnp.broadcast_to(are_ref[...], (S5_SEGS, N_STATES))
    a_im = jnp.broadcast_to(aim_ref[...], (S5_SEGS, N_STATES))

    @pl.when(pl.program_id(0) == 0)
    def _():
        r_re, r_im = are_ref[...], aim_ref[...]
        p_re, p_im = jnp.ones_like(r_re), jnp.zeros_like(r_re)
        n = seg_len
        while n:
            if n & 1:
                p_re, p_im = _cmul(p_re, p_im, r_re, r_im)
            r_re, r_im = _cmul(r_re, r_im, r_re, r_im)
            n >>= 1
        rows_re, rows_im = [jnp.zeros_like(p_re)], [jnp.zeros_like(p_re)]
        for s in range(1, S5_SEGS):
            c_re, c_im = _cmul(p_re, p_im, rows_re[-1], rows_im[-1])
            rows_re.append(c_re + end_ref[0, s - 1:s, :])
            rows_im.append(c_im + end_ref[1, s - 1:s, :])
        x0[0] = jnp.concatenate(rows_re, axis=0)
        x0[1] = jnp.concatenate(rows_im, axis=0)
        pw[0] = a_re
        pw[1] = a_im

    x0_re = x0[0]
    x0_im = x0[1]
    p_re, p_im = pw[0], pw[1]
    for k in range(S5_STEPS):
        rows = slice(k * S5_SEGS, (k + 1) * S5_SEGS)
        zre[rows, :], zim[rows, :] = _cmul(p_re, p_im, x0_re, x0_im)
        p_re, p_im = _cmul(a_re, a_im, p_re, p_im)
    pw[0] = p_re
    pw[1] = p_im
    y = y_ref[...] + (jnp.dot(zre[...].astype(BF16), cre_ref[...], preferred_element_type=F32)
                      - jnp.dot(zim[...].astype(BF16), cim_ref[...], preferred_element_type=F32))
    y = y + d_ref[...] * u_ref[...]
    y = 0.5 * y * (1.0 + jnp.tanh(math.sqrt(2.0 / math.pi) * (y + 0.044715 * (y * y * y))))
    hg = jnp.dot(y.astype(BF16), wglu_ref[...], preferred_element_type=F32)
    o_ref[...] = (hg[:, :SSM_WIDTH] / (1.0 + jnp.exp(-hg[:, SSM_WIDTH:]))).astype(o_ref.dtype)


def _s5_mixer(u, lam_re, lam_im, log_dt, b_re, b_im, c_re, c_im, d_skip, w_glu):
    s = u.shape[0]
    seg_len = s // S5_SEGS
    rows = S5_STEPS * S5_SEGS
    a_re, a_im, bb_re, bb_im = _s5_discretize(lam_re, lam_im, log_dt, b_re, b_im)
    a_re = a_re.reshape(1, N_STATES)
    a_im = a_im.reshape(1, N_STATES)
    eye = jnp.eye(SSM_GROUPS, dtype=F32)

    def in_matrix(bb):
        bb = bb.reshape(SSM_GROUPS, SSM_STATE, SSM_GROUP_CH)
        return jnp.einsum('gpc,gh->gchp', bb, eye).reshape(SSM_WIDTH, N_STATES).astype(BF16)

    def out_matrix(c):
        return jnp.einsum('gcp,gh->gphc', c.astype(F32), eye).reshape(N_STATES, SSM_WIDTH).astype(BF16)

    w_re, w_im = in_matrix(bb_re), in_matrix(bb_im)
    cm_re, cm_im = out_matrix(c_re), out_matrix(c_im)
    d = d_skip.reshape(1, SSM_WIDTH).astype(F32)
    wg = w_glu.astype(BF16)
    u_seg = u.reshape(S5_SEGS, seg_len, SSM_WIDTH).transpose(1, 0, 2).reshape(s, SSM_WIDTH)
    full = lambda a: pl.BlockSpec(a.shape, lambda i: (0,) * a.ndim)
    row = pl.BlockSpec((rows, SSM_WIDTH), lambda i: (i, 0))
    ends = pl.BlockSpec((2, S5_SEGS, N_STATES), lambda i: (0, 0, 0))
    big = pltpu.VMEM((rows, N_STATES), F32)
    small = pltpu.VMEM((2, S5_SEGS, N_STATES), F32)
    y_local, end = pl.pallas_call(
        _s5_local_kernel,
        grid=(seg_len // S5_STEPS,),
        in_specs=[row] + [full(a) for a in (w_re, w_im, a_re, a_im, cm_re, cm_im)],
        out_specs=[row, ends],
        out_shape=[jax.ShapeDtypeStruct((s, SSM_WIDTH), F32),
                   jax.ShapeDtypeStruct((2, S5_SEGS, N_STATES), F32)],
        scratch_shapes=[big, big, small],
        compiler_params=_cparams(("arbitrary",)),
        name="s5_local",
    )(u_seg, w_re, w_im, a_re, a_im, cm_re, cm_im)
    o_seg = pl.pallas_call(
        functools.partial(_s5_finish_kernel, seg_len),
        grid=(seg_len // S5_STEPS,),
        in_specs=[row, row, ends] + [full(a) for a in (a_re, a_im, cm_re, cm_im, d, wg)],
        out_specs=row,
        out_shape=jax.ShapeDtypeStruct((s, SSM_WIDTH), MIX_DTYPE),
        scratch_shapes=[big, big, small, small],
        compiler_params=_cparams(("arbitrary",)),
        name="s5_finish",
    )(y_local, u_seg, end, a_re, a_im, cm_re, cm_im, d, wg)
    return o_seg.reshape(seg_len, S5_SEGS, SSM_WIDTH).transpose(1, 0, 2).reshape(s, SSM_WIDTH)


def _mix_core(osb_ref, oca_ref, ossm_ref, x_ref, g_ref, w_ref, nf_ref):
    g = g_ref[...]
    mixed = jnp.concatenate([
        _rms(osb_ref[...].astype(F32), g[:, :SB_WIDTH]),
        _rms(oca_ref[...].astype(F32), g[:, SB_WIDTH:SB_WIDTH + CA_WIDTH]),
        _rms(ossm_ref[...].astype(F32), g[:, SB_WIDTH + CA_WIDTH:]),
    ], axis=-1)
    x1 = x_ref[...] + jnp.dot(mixed.astype(BF16), w_ref[...], preferred_element_type=F32)
    return x1, _rms(x1, nf_ref[...])


def _mix_dense_kernel(osb_ref, oca_ref, ossm_ref, x_ref, g_ref, w_ref, nf_ref, x1_o, h2_o):
    x1, h2 = _mix_core(osb_ref, oca_ref, ossm_ref, x_ref, g_ref, w_ref, nf_ref)
    x1_o[...] = x1
    h2_o[...] = h2.astype(BF16)


def _mix_moe_kernel(osb_ref, oca_ref, ossm_ref, x_ref, g_ref, w_ref, nf_ref, rboth_ref, rhi_ref,
                    x1_o, h2_o, route_o):
    x1, h2 = _mix_core(osb_ref, oca_ref, ossm_ref, x_ref, g_ref, w_ref, nf_ref)
    x1_o[...] = x1
    h2_o[...] = h2
    hi, lo = _split_bf16(h2)
    nt = (((1,), (1,)), ((), ()))
    both = lax.dot_general(rboth_ref[...], hi, nt, preferred_element_type=F32)
    cross = lax.dot_general(rhi_ref[...], lo, nt, preferred_element_type=F32)
    lg = both[:N_EXPERTS] + both[N_EXPERTS:] + cross[:N_EXPERTS]
    idx = lax.broadcasted_iota(jnp.int32, lg.shape, 0)
    neg = -jnp.inf
    m1 = jnp.max(lg, axis=0, keepdims=True)
    i1 = jnp.min(jnp.where(lg == m1, idx, N_EXPERTS), axis=0, keepdims=True)
    lg2 = jnp.where(idx == i1, neg, lg)
    m2 = jnp.max(lg2, axis=0, keepdims=True)
    i2 = jnp.min(jnp.where(lg2 == m2, idx, N_EXPERTS), axis=0, keepdims=True)
    e = jnp.exp(m2 - m1)
    g1 = 1.0 / (1.0 + e)
    g2 = e / (1.0 + e)
    route_o[...] = jnp.where(idx == 0, i1.astype(F32),
                             jnp.where(idx == 1, i2.astype(F32),
                                       jnp.where(idx == 2, g1, jnp.where(idx == 3, g2, 0.0))))


def _mix_out(o_sb, o_ca, o_ssm, x, g, w, nf, router=None):
    s = x.shape[0]
    row = lambda w_: pl.BlockSpec((TM, w_), lambda i: (i, 0))
    full = lambda a: pl.BlockSpec(a.shape, lambda i: (0, 0))
    ins = [o_sb, o_ca, o_ssm, x, g, w, nf]
    in_specs = [row(SB_WIDTH), row(CA_WIDTH), row(SSM_WIDTH), row(D_MODEL), full(g), full(w), full(nf)]
    if router is None:
        return pl.pallas_call(
            _mix_dense_kernel, grid=(s // TM,), in_specs=in_specs,
            out_specs=[row(D_MODEL)] * 2,
            out_shape=[jax.ShapeDtypeStruct((s, D_MODEL), F32), jax.ShapeDtypeStruct((s, D_MODEL), BF16)],
            compiler_params=_cparams(("parallel",)), name="mix_out_dense",
        )(*ins)
    return pl.pallas_call(
        _mix_moe_kernel, grid=(s // TM,), in_specs=in_specs + [full(router[0]), full(router[1])],
        out_specs=[row(D_MODEL), row(D_MODEL), pl.BlockSpec((N_EXPERTS, TM), lambda i: (0, i))],
        out_shape=[jax.ShapeDtypeStruct((s, D_MODEL), F32)] * 2 + [jax.ShapeDtypeStruct((N_EXPERTS, s), F32)],
        compiler_params=_cparams(("parallel",)), name="mix_out_moe",
    )(*ins, *router)


def _swiglu_tile(xb, wg, wu, wd):
    hg = jnp.dot(xb, wg, preferred_element_type=F32)
    hu = jnp.dot(xb, wu, preferred_element_type=F32)
    hid = hg / (1.0 + jnp.exp(-hg)) * hu
    return jnp.dot(hid.astype(BF16), wd, preferred_element_type=F32)


FF_CHUNKS = (1024, 1024, 768)


def _swiglu_chunks(xb, wg_ref, wu_ref, wd_ref):
    assert sum(FF_CHUNKS) == D_FF
    y = None
    start = 0
    for width in FF_CHUNKS:
        cols = slice(start, start + width)
        part = _swiglu_tile(xb, wg_ref[:, cols], wu_ref[:, cols], wd_ref[cols, :])
        y = part if y is None else y + part
        start += width
    return y


def _ffn_kernel(h_ref, x_ref, wg_ref, wu_ref, wd_ref, o_ref):
    o_ref[...] = x_ref[...] + _swiglu_chunks(h_ref[...], wg_ref.at[0], wu_ref.at[0], wd_ref.at[0])


def _ffn_dense(h2, x1, wg, wu, wd, layer):
    s = x1.shape[0]
    row = pl.BlockSpec((TM, D_MODEL), lambda i: (i, 0))
    resident = lambda a: pl.BlockSpec((1,) + a.shape[1:], lambda i: (layer, 0, 0), pipeline_mode=pl.Buffered(1))
    return pl.pallas_call(
        _ffn_kernel,
        grid=(s // TM,),
        in_specs=[row, row, resident(wg), resident(wu), resident(wd)],
        out_specs=row,
        out_shape=jax.ShapeDtypeStruct((s, D_MODEL), F32),
        compiler_params=_cparams(("parallel",)),
        name="ffn_dense",
    )(h2, x1, wg, wu, wd)


def _for_rows(n, fn):
    for r in range(n):
        fn(r)


def _dispatch_kernel(fill_blk, n_used, pos_ref, h_ref, xs_hbm, zbuf, sem, zsem):
    n_blocks = xs_hbm.shape[0] // MOE_BLOCK

    def zero_fill(blk):
        return pltpu.make_async_copy(
            zbuf, xs_hbm.at[pl.ds(pl.multiple_of(blk * MOE_BLOCK, MOE_BLOCK), MOE_BLOCK)], zsem)

    @pl.when(pl.program_id(0) == 0)
    def _():
        zbuf[...] = jnp.zeros_like(zbuf)
        def fill(blk):
            cp = zero_fill(blk)
            cp.start()
            cp.wait()

        for e in range(N_EXPERTS):
            fill(fill_blk[e])

        def tail(blk, c):
            fill(blk)
            return c
        lax.fori_loop(n_used[0], n_blocks, tail, 0)

    def row_copy(r, k):
        return pltpu.make_async_copy(h_ref.at[pl.ds(r, 1)], xs_hbm.at[pl.ds(pos_ref[0, 0, TOP_K * r + k], 1)], sem)

    def start(r):
        for k in range(TOP_K):
            row_copy(r, k).start(priority=k)

    def wait(r):
        for k in range(TOP_K):
            row_copy(r, k).wait()

    _for_rows(TM, start)
    _for_rows(TM, wait)


def _moe_dispatch(h2, pos, fill_blk, n_used, n_blocks):
    s = h2.shape[0]
    grid_spec = pltpu.PrefetchScalarGridSpec(
        num_scalar_prefetch=2,
        grid=(s // TM,),
        in_specs=[pl.BlockSpec((1, 1, TOP_K * TM), lambda i, f, u: (i, 0, 0), memory_space=pltpu.SMEM),
                  pl.BlockSpec((TM, D_MODEL), lambda i, f, u: (i, 0))],
        out_specs=pl.BlockSpec(memory_space=pl.ANY),
        scratch_shapes=[pltpu.VMEM((MOE_BLOCK, D_MODEL), F32), pltpu.SemaphoreType.DMA,
                        pltpu.SemaphoreType.DMA],
    )
    return pl.pallas_call(
        _dispatch_kernel,
        grid_spec=grid_spec,
        out_shape=jax.ShapeDtypeStruct((n_blocks * MOE_BLOCK, D_MODEL), F32),
        compiler_params=_cparams(("arbitrary",)),
        name="moe_dispatch",
    )(fill_blk, n_used, pos.reshape(s // TM, 1, TOP_K * TM), h2)


def _moe_kernel(blk_e, n_used, xs_ref, wg_ref, wu_ref, wd_ref, o_ref):
    b = pl.program_id(0)

    @pl.when(b < n_used[0])
    def _():
        o_ref[...] = _swiglu_chunks(xs_ref[...].astype(BF16), wg_ref.at[0, 0], wu_ref.at[0, 0], wd_ref.at[0, 0])

    @pl.when(b >= n_used[0])
    def _():
        o_ref[...] = jnp.zeros_like(o_ref)


def _moe_experts(xs, blk_e, n_used, wg, wu, wd, layer, n_blocks):
    weights = lambda a: pl.BlockSpec((1, 1) + a.shape[2:], lambda b, e, u: (layer, e[b], 0, 0))
    grid_spec = pltpu.PrefetchScalarGridSpec(
        num_scalar_prefetch=2,
        grid=(n_blocks,),
        in_specs=[pl.BlockSpec((MOE_BLOCK, D_MODEL), lambda b, e, u: (jnp.minimum(b, u[0] - 1), 0)),
                  weights(wg), weights(wu), weights(wd)],
        out_specs=pl.BlockSpec((MOE_BLOCK, D_MODEL), lambda b, e, u: (b, 0)),
    )
    return pl.pallas_call(
        _moe_kernel,
        grid_spec=grid_spec,
        out_shape=jax.ShapeDtypeStruct((n_blocks * MOE_BLOCK, D_MODEL), F32),
        compiler_params=_cparams(("arbitrary",)),
        name="moe_experts",
    )(blk_e, n_used, xs, wg, wu, wd)


def _combine_kernel(pos_ref, x_ref, gate_ref, ys_hbm, o_ref, ybuf, sem):
    def row_copy(r, k):
        return pltpu.make_async_copy(ys_hbm.at[pl.ds(pos_ref[0, 0, TOP_K * r + k], 1)],
                                     ybuf.at[k, pl.ds(r, 1)], sem)

    def start(r):
        for k in range(TOP_K):
            row_copy(r, k).start(priority=k)

    def wait(r):
        for k in range(TOP_K):
            row_copy(r, k).wait()

    _for_rows(TM, start)
    _for_rows(TM, wait)
    gates = gate_ref[...]
    gated = [ybuf[k] * gates[:, k:k + 1] for k in range(TOP_K)]
    o_ref[...] = x_ref[...] + functools.reduce(lambda a, b_: a + b_, gated)


def _moe_combine(x1, gates, pos, ys):
    s = x1.shape[0]
    row = lambda w_: pl.BlockSpec((TM, w_), lambda i: (i, 0))
    return pl.pallas_call(
        _combine_kernel,
        grid=(s // TM,),
        in_specs=[pl.BlockSpec((1, 1, TOP_K * TM), lambda i: (i, 0, 0), memory_space=pltpu.SMEM),
                  row(D_MODEL), row(TOP_K), pl.BlockSpec(memory_space=pl.ANY)],
        out_specs=row(D_MODEL),
        out_shape=jax.ShapeDtypeStruct((s, D_MODEL), F32),
        scratch_shapes=[pltpu.VMEM((TOP_K, TM, D_MODEL), F32), pltpu.SemaphoreType.DMA],
        compiler_params=_cparams(("arbitrary",)),
        name="moe_combine",
    )(pos.reshape(s // TM, 1, TOP_K * TM), x1, gates, ys)


def _moe_layer(x1, h2, route, wg, wu, wd, layer):
    s = h2.shape[0]
    n_blocks = s * TOP_K // MOE_BLOCK + N_EXPERTS
    flat_e = route[:TOP_K].T.astype(jnp.int32).reshape(-1)
    gates = route[TOP_K:2 * TOP_K].T
    onehot = (flat_e[:, None] == jnp.arange(N_EXPERTS)[None, :]).astype(jnp.int32)
    csum = jnp.cumsum(onehot, axis=0)
    counts = csum[-1]
    padded = (counts + MOE_BLOCK - 1) // MOE_BLOCK * MOE_BLOCK
    pend = jnp.cumsum(padded)
    pstart = pend - padded
    pos = jnp.sum(onehot * (csum - 1 + pstart[None, :]), axis=1).astype(jnp.int32)
    blk_start = jnp.arange(n_blocks, dtype=jnp.int32) * MOE_BLOCK
    blk_e = jnp.minimum(jnp.sum((pend[None, :] <= blk_start[:, None]).astype(jnp.int32), axis=1),
                        N_EXPERTS - 1).astype(jnp.int32)
    n_used = (pend[-1] // MOE_BLOCK).astype(jnp.int32).reshape(1)
    fill_blk = jnp.maximum(pend // MOE_BLOCK - 1, 0).astype(jnp.int32)
    xs = _moe_dispatch(h2, pos, fill_blk, n_used, n_blocks)
    ys = _moe_experts(xs, blk_e, n_used, wg, wu, wd, layer, n_blocks)
    return _moe_combine(x1, gates, pos, ys)


def _block_diag_ones(width, group):
    idx = jnp.arange(width) // group
    return (idx[:, None] == idx[None, :]).astype(BF16)


def kernel(x, norm_mix, w_in, q_norm_sb, k_norm_sb, q_norm_ca, k_norm_ca, rel_bias, lam_re, lam_im, log_dt, b_re, b_im, c_re, c_im, d_skip, w_glu, g_out, w_out, norm_ffn, w_gate_dense, w_up_dense, w_down_dense, w_router, w_gate_moe, w_up_moe, w_down_moe):
    b, s, d = x.shape
    assert b == 1 and d == D_MODEL and s % (TM * 2) == 0
    xs = x.reshape(s, d).astype(F32)
    scale = 1.0 / math.sqrt(HEAD_DIM)
    gm = _block_diag_ones(GROUP_SUM_WIDTH, HEAD_DIM)
    idx = jnp.arange(SB_BLOCK)
    tri = (idx[:, None] > idx[None, :]).astype(BF16)
    vec = lambda a: a.reshape(1, -1).astype(F32)
    tile = lambda a, width: jnp.tile(a.astype(F32), width // HEAD_DIM).reshape(1, width)

    dense_w = [w.astype(BF16) for w in (w_gate_dense, w_up_dense, w_down_dense)]
    moe_w = [w.astype(BF16) for w in (w_gate_moe, w_up_moe, w_down_moe)]
    ca_bias = _ca_bias_mask(rel_bias.reshape(DEPTH * CA_HEADS, 2 * REL_CLIP + 1))

    for l in range(DEPTH):
        q_sb, k_sb, v_sb, q_ca, k_ca, v_ca, u = _in_proj(
            xs, vec(norm_mix[l]), w_in[l].astype(BF16),
            tile(q_norm_sb[l], SB_WIDTH) * (scale * LOG2E), tile(k_norm_sb[l], SB_WIDTH),
            tile(q_norm_ca[l], CA_WIDTH) * scale, tile(k_norm_ca[l], CA_WIDTH), gm)
        o_sb = _sb_attention(q_sb, k_sb, v_sb, tri)
        o_ca = _ca_attention(q_ca, k_ca, v_ca, ca_bias, layer=l)
        o_ssm = _s5_mixer(u, lam_re[l], lam_im[l], log_dt[l], b_re[l], b_im[l], c_re[l], c_im[l],
                          d_skip[l], w_glu[l])
        i = l // 2
        if l % 2 == 0:
            x1, h2 = _mix_out(o_sb, o_ca, o_ssm, xs, vec(g_out[l]), w_out[l].astype(BF16), vec(norm_ffn[l]))
            xs = _ffn_dense(h2, x1, *dense_w, layer=i)
        else:
            r_hi, r_lo = _split_bf16(w_router[i].astype(F32).T)
            r_both = jnp.concatenate([r_hi, r_lo], axis=0)
            r_hi = jnp.concatenate([r_hi, jnp.zeros_like(r_hi)], axis=0)
            x1, h2, route = _mix_out(o_sb, o_ca, o_ssm, xs, vec(g_out[l]), w_out[l].astype(BF16),
                                     vec(norm_ffn[l]), router=(r_both, r_hi))
            xs = _moe_layer(x1, h2, route, *moe_w, layer=i)
    return xs.reshape(b, s, d).astype(x.dtype)
```

```python
import functools
import math

import jax
import jax.numpy as jnp
import numpy as np
from jax import lax
from jax.experimental import pallas as pl
from jax.experimental.pallas import tpu as pltpu

F32 = jnp.float32
BF16 = jnp.bfloat16
MIX_DTYPE = BF16

D_MODEL = 1024
DEPTH = 4
CHUNK = 64
HEAD_DIM = 64
SB_WIDTH = 256
CA_WIDTH = 512
CA_HEADS = 8
SSM_WIDTH = 256
SSM_GROUP_CH = 16
SSM_GROUPS = 16
SSM_STATE = 64
N_STATES = SSM_GROUPS * SSM_STATE
D_IN_PROJ = 3 * SB_WIDTH + 3 * CA_WIDTH + SSM_WIDTH
MASK_LOGIT = -1e4
LOOKBACK = 8
REL_CLIP = 128
D_FF = 2816
N_EXPERTS = 8
TOP_K = 2
RMS_EPS = 1e-6

LANES = 128
SUBLANES = 8
VMEM_LIMIT = 56 * 1024 * 1024

TM = 512
SB_BLOCK = 256
SB_BLOCKS_PER_ITER = 2
CA_BLOCK = 256
MOE_BLOCK = 512
SB_UNDERFLOW = 104.0
LOG2E = math.log2(math.e)


def _cparams(sem):
    return pltpu.CompilerParams(dimension_semantics=sem, vmem_limit_bytes=VMEM_LIMIT)


def _split_bf16(x):
    hi = x.astype(BF16)
    lo = (x - hi.astype(F32)).astype(BF16)
    return hi, lo


def _rms(x, gain):
    ms = jnp.mean(x * x, axis=-1, keepdims=True)
    return x * lax.rsqrt(ms + RMS_EPS) * gain


GROUP_SUM_WIDTH = 256


def _head_rms(p, gmat, gain):
    sq = (p * p).astype(BF16)
    blocks = []
    for c in range(p.shape[1] // GROUP_SUM_WIDTH):
        cols = slice(c * GROUP_SUM_WIDTH, (c + 1) * GROUP_SUM_WIDTH)
        blocks.append(jnp.dot(sq[:, cols], gmat, preferred_element_type=F32))
    ss = blocks[0] if len(blocks) == 1 else jnp.concatenate(blocks, axis=1)
    return p * lax.rsqrt(ss * (1.0 / HEAD_DIM) + RMS_EPS) * gain


def _in_proj_kernel(x_ref, g_ref, w_ref, gq_sb, gk_sb, gq_ca, gk_ca, gm_ref,
                    qsb_o, ksb_o, vsb_o, qca_o, kca_o, vca_o, u_o):
    h = _rms(x_ref[...], g_ref[...])
    proj = jnp.dot(h.astype(BF16), w_ref[...], preferred_element_type=F32)
    gm = gm_ref[...]
    o1, o2, o3 = SB_WIDTH, 2 * SB_WIDTH, 3 * SB_WIDTH
    o4, o5, o6 = o3 + CA_WIDTH, o3 + 2 * CA_WIDTH, o3 + 3 * CA_WIDTH
    qsb_o[...] = _head_rms(proj[:, :o1], gm, gq_sb[...]).astype(BF16)
    ksb_o[...] = _head_rms(proj[:, o1:o2], gm, gk_sb[...]).astype(BF16)
    vsb_o[...] = proj[:, o2:o3].astype(BF16)
    qca_o[...] = _head_rms(proj[:, o3:o4], gm, gq_ca[...]).astype(BF16)
    kca_o[...] = _head_rms(proj[:, o4:o5], gm, gk_ca[...]).astype(BF16)
    vca_o[...] = proj[:, o5:o6].astype(BF16)
    u_o[...] = proj[:, o6:]


def _in_proj(x, g, w, gq_sb, gk_sb, gq_ca, gk_ca, gm):
    s = x.shape[0]
    row = lambda w_: pl.BlockSpec((TM, w_), lambda i: (i, 0))
    full = lambda a: pl.BlockSpec(a.shape, lambda i: (0,) * a.ndim)
    return pl.pallas_call(
        _in_proj_kernel,
        grid=(s // TM,),
        in_specs=[row(D_MODEL)] + [full(a) for a in (g, w, gq_sb, gk_sb, gq_ca, gk_ca, gm)],
        out_specs=[row(SB_WIDTH)] * 3 + [row(CA_WIDTH)] * 3 + [row(SSM_WIDTH)],
        out_shape=[jax.ShapeDtypeStruct((s, SB_WIDTH), BF16)] * 3
        + [jax.ShapeDtypeStruct((s, CA_WIDTH), BF16)] * 3
        + [jax.ShapeDtypeStruct((s, SSM_WIDTH), F32)],
        compiler_params=_cparams(("parallel",)),
        name="in_proj",
    )(x, g, w, gq_sb, gk_sb, gq_ca, gk_ca, gm)


def _sb_kernel(q_ref, k_ref, v_ref, tri_ref, o_ref):
    i = pl.program_id(0)
    tri = tri_ref[...]
    row = lax.broadcasted_iota(jnp.int32, (SB_BLOCK, SB_BLOCK), 0)
    col = lax.broadcasted_iota(jnp.int32, (SB_BLOCK, SB_BLOCK), 1)
    rel = col - row
    lane = lax.broadcasted_iota(jnp.int32, (SB_BLOCK, LANES), 1)
    n_pairs = SB_WIDTH // LANES
    per_pair = LANES // HEAD_DIM
    in_head = [(lane >= hh * HEAD_DIM) & (lane < (hh + 1) * HEAD_DIM) for hh in range(per_pair)]
    qm = [jnp.where(in_head[hh], q_ref[:, p * LANES:(p + 1) * LANES].astype(F32), 0.0).astype(BF16)
          for p in range(n_pairs) for hh in range(per_pair)]

    def cond(c):
        j, done = c[0], c[1]
        return jnp.logical_and(j >= 0, jnp.logical_not(done))

    def body(c):
        j, _, carries, accs = c
        heads = range(n_pairs * per_pair)
        lanes = [slice(p * LANES, (p + 1) * LANES) for p in range(n_pairs)]
        add = lambda a, b_: a + b_
        blocks = range(SB_BLOCKS_PER_ITER)
        starts = [pl.multiple_of(jnp.maximum(j - b_, 0) * SB_BLOCK, SB_BLOCK) for b_ in blocks]
        visible = [jnp.logical_and(rel < (i - j + b_) * SB_BLOCK, j - b_ >= 0) for b_ in blocks]
        kb = [[k_ref[pl.ds(starts[b_], SB_BLOCK), lanes[p]] for p in range(n_pairs)] for b_ in blocks]
        vb = [[v_ref[pl.ds(starts[b_], SB_BLOCK), lanes[p]] for p in range(n_pairs)] for b_ in blocks]
        tiles = [(b_, h) for b_ in blocks for h in heads]
        z = {t: lax.dot_general(qm[t[1]], kb[t[0]][t[1] // per_pair], (((1,), (1,)), ((), ())),
                                preferred_element_type=F32) for t in tiles}
        z = {t: jnp.where(visible[t[0]], z[t], MASK_LOGIT * LOG2E) for t in tiles}
        sp = {t: jnp.maximum(z[t], 0.0) + jnp.log2(1.0 + jnp.exp2(-jnp.abs(z[t]))) for t in tiles}
        later = {t: jnp.dot(sp[t].astype(BF16), tri, preferred_element_type=F32) for t in tiles}
        total = {t: jnp.sum(sp[t], axis=-1, keepdims=True) for t in tiles}
        carry = {}
        for h in heads:
            running = carries[h]
            for b_ in blocks:
                carry[b_, h] = running
                running = running + total[b_, h]
            carry[SB_BLOCKS_PER_ITER, h] = running
        w = {t: jnp.exp2(z[t] - sp[t] - (later[t] + carry[t])).astype(BF16) for t in tiles}
        pv = {t: jnp.dot(w[t], vb[t[0]][t[1] // per_pair], preferred_element_type=F32) for t in tiles}
        new_accs = [accs[p] + functools.reduce(
            add, [jnp.where(in_head[hh], functools.reduce(add, [pv[b_, p * per_pair + hh] for b_ in blocks]), 0.0)
                  for hh in range(per_pair)])
            for p in range(n_pairs)]
        new_carries = [carry[SB_BLOCKS_PER_ITER, h] for h in heads]
        done = jnp.min(functools.reduce(jnp.minimum, new_carries)) > SB_UNDERFLOW * LOG2E
        return j - SB_BLOCKS_PER_ITER, done, tuple(new_carries), tuple(new_accs)

    init = (i, False,
            tuple(jnp.zeros((SB_BLOCK, 1), F32) for _ in range(n_pairs * per_pair)),
            tuple(jnp.zeros((SB_BLOCK, LANES), F32) for _ in range(n_pairs)))
    accs = lax.while_loop(cond, body, init)[3]
    for p in range(n_pairs):
        o_ref[:, p * LANES:(p + 1) * LANES] = accs[p].astype(o_ref.dtype)


def _sb_attention(q, k, v, tri):
    s = q.shape[0]
    whole = pl.BlockSpec(memory_space=pltpu.VMEM)
    return pl.pallas_call(
        _sb_kernel,
        grid=(s // SB_BLOCK,),
        in_specs=[pl.BlockSpec((SB_BLOCK, SB_WIDTH), lambda i: (i, 0)), whole, whole, whole],
        out_specs=pl.BlockSpec((SB_BLOCK, SB_WIDTH), lambda i: (i, 0)),
        out_shape=jax.ShapeDtypeStruct((s, SB_WIDTH), MIX_DTYPE),
        compiler_params=_cparams(("parallel",)),
        name="sb_attention",
    )(q, k, v, tri)


CA_WINDOW = 3
CA_PAIRS_PER_GROUP = 2


def _ca_kernel(q_ref, k0, k1, k2, v0, v1, v2, bm_ref, o_ref):
    i = pl.program_id(0)
    k_refs = (k0, k1, k2)
    v_refs = (v0, v1, v2)
    lane = lax.broadcasted_iota(jnp.int32, (CA_BLOCK, LANES), 1)
    per_pair = LANES // HEAD_DIM
    in_head = [(lane >= hh * HEAD_DIM) & (lane < (hh + 1) * HEAD_DIM) for hh in range(per_pair)]
    add = lambda a, b_: a + b_
    tiles = range(CA_WINDOW)
    for group in range(CA_WIDTH // LANES // CA_PAIRS_PER_GROUP):
        pairs = [group * CA_PAIRS_PER_GROUP + g for g in range(CA_PAIRS_PER_GROUP)]
        heads = [(p, hh) for p in pairs for hh in range(per_pair)]
        lanes = {p: slice(p * LANES, (p + 1) * LANES) for p in pairs}
        qm = {(p, hh): jnp.where(in_head[hh], q_ref[:, lanes[p]].astype(F32), 0.0).astype(BF16)
              for p, hh in heads}
        sc = {(hd, jj): lax.dot_general(qm[hd], k_refs[jj][:, lanes[hd[0]]], (((1,), (1,)), ((), ())),
                                        preferred_element_type=F32)
              for hd in heads for jj in tiles}
        sc = {(hd, jj): jnp.where(i + jj >= CA_WINDOW - 1,
                                  sc[hd, jj] + bm_ref[hd[0] * per_pair + hd[1], :, jj * CA_BLOCK:(jj + 1) * CA_BLOCK],
                                  -1e30)
              for hd in heads for jj in tiles}
        m = {hd: functools.reduce(jnp.maximum, [jnp.max(sc[hd, jj], axis=-1, keepdims=True) for jj in tiles])
             for hd in heads}
        p_un = {(hd, jj): jnp.exp(sc[hd, jj] - m[hd]) for hd in heads for jj in tiles}
        denom = {hd: functools.reduce(add, [jnp.sum(p_un[hd, jj], axis=-1, keepdims=True) for jj in tiles])
                 for hd in heads}
        pv = {hd: functools.reduce(add, [jnp.dot(p_un[hd, jj].astype(BF16), v_refs[jj][:, lanes[hd[0]]],
                                                 preferred_element_type=F32) for jj in tiles])
              for hd in heads}
        for p in pairs:
            o_ref[:, lanes[p]] = functools.reduce(
                add, [jnp.where(in_head[hh], pv[p, hh] / denom[p, hh], 0.0) for hh in range(per_pair)]
            ).astype(o_ref.dtype)


def _ca_attention(q, k, v, bm, layer):
    s = q.shape[0]
    blk = lambda off: pl.BlockSpec((CA_BLOCK, CA_WIDTH), lambda i: (jnp.maximum(i - off, 0), 0))
    return pl.pallas_call(
        _ca_kernel,
        grid=(s // CA_BLOCK,),
        in_specs=[blk(0), blk(2), blk(1), blk(0), blk(2), blk(1), blk(0),
                  pl.BlockSpec((CA_HEADS,) + bm.shape[1:], lambda i: (layer, 0, 0))],
        out_specs=blk(0),
        out_shape=jax.ShapeDtypeStruct((s, CA_WIDTH), MIX_DTYPE),
        compiler_params=_cparams(("parallel",)),
        name="ca_attention",
    )(q, k, k, k, v, v, v, bm)


CA_BIAS_SPAN = 1024


def _ca_bias_kernel(g_ref, o_ref):
    width = CA_WINDOW * CA_BLOCK
    rows = jnp.broadcast_to(g_ref[0], (CA_BLOCK, CA_BIAS_SPAN))
    toeplitz = pltpu.roll(rows, 0, 1, stride=1, stride_axis=0)[:, :width]
    r = lax.broadcasted_iota(jnp.int32, (CA_BLOCK, width), 0)
    c = lax.broadcasted_iota(jnp.int32, (CA_BLOCK, width), 1)
    back = r // CHUNK - (c // CHUNK - LOOKBACK)
    o_ref[0] = jnp.where((back >= 0) & (back <= LOOKBACK), toeplitz, -1e30)


def _ca_bias_mask(rel_bias):
    planes = rel_bias.shape[0]
    width = CA_WINDOW * CA_BLOCK
    assert CA_BLOCK + width - 1 <= CA_BIAS_SPAN
    m = np.arange(CA_BIAS_SPAN)
    offset = np.where(m < width, m, m - CA_BIAS_SPAN)
    dist = LOOKBACK * CHUNK - offset
    by_offset = rel_bias.astype(F32)[:, np.clip(dist, -REL_CLIP, REL_CLIP) + REL_CLIP]
    return pl.pallas_call(
        _ca_bias_kernel,
        grid=(planes,),
        in_specs=[pl.BlockSpec((1, 1, CA_BIAS_SPAN), lambda i: (i, 0, 0))],
        out_specs=pl.BlockSpec((1, CA_BLOCK, width), lambda i: (i, 0, 0)),
        out_shape=jax.ShapeDtypeStruct((planes, CA_BLOCK, width), F32),
        compiler_params=_cparams(("parallel",)),
        name="ca_bias",
    )(by_offset.reshape(planes, 1, CA_BIAS_SPAN))


def _s5_disc_kernel(lre_ref, lim_ref, ldt_ref, bre_ref, bim_ref, are_o, aim_o, bbre_o, bbim_o):
    lre = lre_ref[...]
    lim = lim_ref[...]
    dt = jnp.exp(ldt_ref[...])
    mag = jnp.exp(lre * dt)
    ang = lim * dt
    a_re = mag * jnp.cos(ang)
    a_im = mag * jnp.sin(ang)
    are_o[...] = a_re
    aim_o[...] = a_im
    nre = a_re - 1.0
    nim = a_im
    den = lre * lre + lim * lim
    zoh_re = (nre * lre + nim * lim) / den
    zoh_im = (nim * lre - nre * lim) / den
    bre = bre_ref[...]
    bim = bim_ref[...]
    bbre_o[...] = zoh_re * bre - zoh_im * bim
    bbim_o[...] = zoh_re * bim + zoh_im * bre


def _s5_discretize(lam_re, lam_im, log_dt, b_re, b_im):
    col = lambda a: a.reshape(N_STATES, 1).astype(F32)
    ldt = jnp.repeat(log_dt.astype(F32), SSM_STATE).reshape(N_STATES, 1)
    b2 = lambda a: a.reshape(N_STATES, SSM_GROUP_CH).astype(F32)
    return pl.pallas_call(
        _s5_disc_kernel,
        out_shape=[jax.ShapeDtypeStruct((N_STATES, 1), F32)] * 2
        + [jax.ShapeDtypeStruct((N_STATES, SSM_GROUP_CH), F32)] * 2,
        name="s5_discretize",
    )(col(lam_re), col(lam_im), ldt, b2(b_re), b2(b_im))


S5_SEGS = SUBLANES
S5_STEPS = 128


def _cmul(a_re, a_im, b_re, b_im):
    return a_re * b_re - a_im * b_im, a_re * b_im + a_im * b_re


def _s5_local_kernel(u_ref, wre_ref, wim_ref, are_ref, aim_ref, cre_ref, cim_ref, y_o, end_o,
                     xre, xim, st):
    @pl.when(pl.program_id(0) == 0)
    def _():
        st[...] = jnp.zeros_like(st)

    u = u_ref[...].astype(BF16)
    xre[...] = jnp.dot(u, wre_ref[...], preferred_element_type=F32)
    xim[...] = jnp.dot(u, wim_ref[...], preferred_element_type=F32)
    a_re = jnp.broadcast_to(are_ref[...], (S5_SEGS, N_STATES))
    a_im = jnp.broadcast_to(aim_ref[...], (S5_SEGS, N_STATES))
    x_re, x_im = st[0], st[1]
    for k in range(S5_STEPS):
        rows = slice(k * S5_SEGS, (k + 1) * S5_SEGS)
        p_re, p_im = _cmul(a_re, a_im, x_re, x_im)
        x_re = p_re + xre[rows, :]
        x_im = p_im + xim[rows, :]
        xre[rows, :] = x_re
        xim[rows, :] = x_im
    st[0] = x_re
    st[1] = x_im
    end_o[0] = x_re
    end_o[1] = x_im
    y_o[...] = (jnp.dot(xre[...].astype(BF16), cre_ref[...], preferred_element_type=F32)
                - jnp.dot(xim[...].astype(BF16), cim_ref[...], preferred_element_type=F32))


def _s5_finish_kernel(seg_len, y_ref, u_ref, end_ref, are_ref, aim_ref, cre_ref, cim_ref, d_ref, wglu_ref,
                      o_ref, zre, zim, x0, pw):
    a_re = jnp.broadcast_to(are_ref[...], (S5_SEGS, N_STATES))
    a_im = jnp.broadcast_to(aim_ref[...], (S5_SEGS, N_STATES))

    @pl.when(pl.program_id(0) == 0)
    def _():
        r_re, r_im = are_ref[...], aim_ref[...]
        p_re, p_im = jnp.ones_like(r_re), jnp.zeros_like(r_re)
        n = seg_len
        while n:
            if n & 1:
                p_re, p_im = _cmul(p_re, p_im, r_re, r_im)
            r_re, r_im = _cmul(r_re, r_im, r_re, r_im)
            n >>= 1
        rows_re, rows_im = [jnp.zeros_like(p_re)], [jnp.zeros_like(p_re)]
        for s in range(1, S5_SEGS):
            c_re, c_im = _cmul(p_re, p_im, rows_re[-1], rows_im[-1])
            rows_re.append(c_re + end_ref[0, s - 1:s, :])
            rows_im.append(c_im + end_ref[1, s - 1:s, :])
        x0[0] = jnp.concatenate(rows_re, axis=0)
        x0[1] = jnp.concatenate(rows_im, axis=0)
        pw[0] = a_re
        pw[1] = a_im

    x0_re = x0[0]
    x0_im = x0[1]
    p_re, p_im = pw[0], pw[1]
    for k in range(S5_STEPS):
        rows = slice(k * S5_SEGS, (k + 1) * S5_SEGS)
        zre[rows, :], zim[rows, :] = _cmul(p_re, p_im, x0_re, x0_im)
        p_re, p_im = _cmul(a_re, a_im, p_re, p_im)
    pw[0] = p_re
    pw[1] = p_im
    y = y_ref[...] + (jnp.dot(zre[...].astype(BF16), cre_ref[...], preferred_element_type=F32)
                      - jnp.dot(zim[...].astype(BF16), cim_ref[...], preferred_element_type=F32))
    y = y + d_ref[...] * u_ref[...]
    y = 0.5 * y * (1.0 + jnp.tanh(math.sqrt(2.0 / math.pi) * (y + 0.044715 * (y * y * y))))
    hg = jnp.dot(y.astype(BF16), wglu_ref[...], preferred_element_type=F32)
    o_ref[...] = (hg[:, :SSM_WIDTH] / (1.0 + jnp.exp(-hg[:, SSM_WIDTH:]))).astype(o_ref.dtype)


def _s5_mixer(u, lam_re, lam_im, log_dt, b_re, b_im, c_re, c_im, d_skip, w_glu):
    s = u.shape[0]
    seg_len = s // S5_SEGS
    rows = S5_STEPS * S5_SEGS
    a_re, a_im, bb_re, bb_im = _s5_discretize(lam_re, lam_im, log_dt, b_re, b_im)
    a_re = a_re.reshape(1, N_STATES)
    a_im = a_im.reshape(1, N_STATES)
    eye = jnp.eye(SSM_GROUPS, dtype=F32)

    def in_matrix(bb):
        bb = bb.reshape(SSM_GROUPS, SSM_STATE, SSM_GROUP_CH)
        return jnp.einsum('gpc,gh->gchp', bb, eye).reshape(SSM_WIDTH, N_STATES).astype(BF16)

    def out_matrix(c):
        return jnp.einsum('gcp,gh->gphc', c.astype(F32), eye).reshape(N_STATES, SSM_WIDTH).astype(BF16)

    w_re, w_im = in_matrix(bb_re), in_matrix(bb_im)
    cm_re, cm_im = out_matrix(c_re), out_matrix(c_im)
    d = d_skip.reshape(1, SSM_WIDTH).astype(F32)
    wg = w_glu.astype(BF16)
    u_seg = u.reshape(S5_SEGS, seg_len, SSM_WIDTH).transpose(1, 0, 2).reshape(s, SSM_WIDTH)
    full = lambda a: pl.BlockSpec(a.shape, lambda i: (0,) * a.ndim)
    row = pl.BlockSpec((rows, SSM_WIDTH), lambda i: (i, 0))
    ends = pl.BlockSpec((2, S5_SEGS, N_STATES), lambda i: (0, 0, 0))
    big = pltpu.VMEM((rows, N_STATES), F32)
    small = pltpu.VMEM((2, S5_SEGS, N_STATES), F32)
    y_local, end = pl.pallas_call(
        _s5_local_kernel,
        grid=(seg_len // S5_STEPS,),
        in_specs=[row] + [full(a) for a in (w_re, w_im, a_re, a_im, cm_re, cm_im)],
        out_specs=[row, ends],
        out_shape=[jax.ShapeDtypeStruct((s, SSM_WIDTH), F32),
                   jax.ShapeDtypeStruct((2, S5_SEGS, N_STATES), F32)],
        scratch_shapes=[big, big, small],
        compiler_params=_cparams(("arbitrary",)),
        name="s5_local",
    )(u_seg, w_re, w_im, a_re, a_im, cm_re, cm_im)
    o_seg = pl.pallas_call(
        functools.partial(_s5_finish_kernel, seg_len),
        grid=(seg_len // S5_STEPS,),
        in_specs=[row, row, ends] + [full(a) for a in (a_re, a_im, cm_re, cm_im, d, wg)],
        out_specs=row,
        out_shape=jax.ShapeDtypeStruct((s, SSM_WIDTH), MIX_DTYPE),
        scratch_shapes=[big, big, small, small],
        compiler_params=_cparams(("arbitrary",)),
        name="s5_finish",
    )(y_local, u_seg, end, a_re, a_im, cm_re, cm_im, d, wg)
    return o_seg.reshape(seg_len, S5_SEGS, SSM_WIDTH).transpose(1, 0, 2).reshape(s, SSM_WIDTH)


def _mix_core(osb_ref, oca_ref, ossm_ref, x_ref, g_ref, w_ref, nf_ref):
    g = g_ref[...]
    mixed = jnp.concatenate([
        _rms(osb_ref[...].astype(F32), g[:, :SB_WIDTH]),
        _rms(oca_ref[...].astype(F32), g[:, SB_WIDTH:SB_WIDTH + CA_WIDTH]),
        _rms(ossm_ref[...].astype(F32), g[:, SB_WIDTH + CA_WIDTH:]),
    ], axis=-1)
    x1 = x_ref[...] + jnp.dot(mixed.astype(BF16), w_ref[...], preferred_element_type=F32)
    return x1, _rms(x1, nf_ref[...])


def _mix_dense_kernel(osb_ref, oca_ref, ossm_ref, x_ref, g_ref, w_ref, nf_ref, x1_o, h2_o):
    x1, h2 = _mix_core(osb_ref, oca_ref, ossm_ref, x_ref, g_ref, w_ref, nf_ref)
    x1_o[...] = x1
    h2_o[...] = h2.astype(BF16)


def _mix_moe_kernel(osb_ref, oca_ref, ossm_ref, x_ref, g_ref, w_ref, nf_ref, rboth_ref, rhi_ref,
                    x1_o, h2_o, route_o):
    x1, h2 = _mix_core(osb_ref, oca_ref, ossm_ref, x_ref, g_ref, w_ref, nf_ref)
    x1_o[...] = x1
    h2_o[...] = h2
    hi, lo = _split_bf16(h2)
    nt = (((1,), (1,)), ((), ()))
    both = lax.dot_general(rboth_ref[...], hi, nt, preferred_element_type=F32)
    cross = lax.dot_general(rhi_ref[...], lo, nt, preferred_element_type=F32)
    lg = both[:N_EXPERTS] + both[N_EXPERTS:] + cross[:N_EXPERTS]
    idx = lax.broadcasted_iota(jnp.int32, lg.shape, 0)
    neg = -jnp.inf
    m1 = jnp.max(lg, axis=0, keepdims=True)
    i1 = jnp.min(jnp.where(lg == m1, idx, N_EXPERTS), axis=0, keepdims=True)
    lg2 = jnp.where(idx == i1, neg, lg)
    m2 = jnp.max(lg2, axis=0, keepdims=True)
    i2 = jnp.min(jnp.where(lg2 == m2, idx, N_EXPERTS), axis=0, keepdims=True)
    e = jnp.exp(m2 - m1)
    g1 = 1.0 / (1.0 + e)
    g2 = e / (1.0 + e)
    route_o[...] = jnp.where(idx == 0, i1.astype(F32),
                             jnp.where(idx == 1, i2.astype(F32),
                                       jnp.where(idx == 2, g1, jnp.where(idx == 3, g2, 0.0))))


def _mix_out(o_sb, o_ca, o_ssm, x, g, w, nf, router=None):
    s = x.shape[0]
    row = lambda w_: pl.BlockSpec((TM, w_), lambda i: (i, 0))
    full = lambda a: pl.BlockSpec(a.shape, lambda i: (0, 0))
    ins = [o_sb, o_ca, o_ssm, x, g, w, nf]
    in_specs = [row(SB_WIDTH), row(CA_WIDTH), row(SSM_WIDTH), row(D_MODEL), full(g), full(w), full(nf)]
    if router is None:
        return pl.pallas_call(
            _mix_dense_kernel, grid=(s // TM,), in_specs=in_specs,
            out_specs=[row(D_MODEL)] * 2,
            out_shape=[jax.ShapeDtypeStruct((s, D_MODEL), F32), jax.ShapeDtypeStruct((s, D_MODEL), BF16)],
            compiler_params=_cparams(("parallel",)), name="mix_out_dense",
        )(*ins)
    return pl.pallas_call(
        _mix_moe_kernel, grid=(s // TM,), in_specs=in_specs + [full(router[0]), full(router[1])],
        out_specs=[row(D_MODEL), row(D_MODEL), pl.BlockSpec((N_EXPERTS, TM), lambda i: (0, i))],
        out_shape=[jax.ShapeDtypeStruct((s, D_MODEL), F32)] * 2 + [jax.ShapeDtypeStruct((N_EXPERTS, s), F32)],
        compiler_params=_cparams(("parallel",)), name="mix_out_moe",
    )(*ins, *router)


def _swiglu_tile(xb, wg, wu, wd):
    hg = jnp.dot(xb, wg, preferred_element_type=F32)
    hu = jnp.dot(xb, wu, preferred_element_type=F32)
    hid = hg / (1.0 + jnp.exp(-hg)) * hu
    return jnp.dot(hid.astype(BF16), wd, preferred_element_type=F32)


FF_CHUNKS = (1024, 1024, 768)


def _swiglu_chunks(xb, wg_ref, wu_ref, wd_ref):
    assert sum(FF_CHUNKS) == D_FF
    bf16 = lambda a: a if a.dtype == BF16 else a.astype(BF16)
    y = None
    start = 0
    for width in FF_CHUNKS:
        cols = slice(start, start + width)
        part = _swiglu_tile(xb, bf16(wg_ref[:, cols]), bf16(wu_ref[:, cols]), bf16(wd_ref[cols, :]))
        y = part if y is None else y + part
        start += width
    return y


def _ffn_kernel(h_ref, x_ref, wg_ref, wu_ref, wd_ref, o_ref):
    o_ref[...] = x_ref[...] + _swiglu_chunks(h_ref[...], wg_ref.at[0], wu_ref.at[0], wd_ref.at[0])


def _ffn_dense(h2, x1, wg, wu, wd, layer):
    s = x1.shape[0]
    row = pl.BlockSpec((TM, D_MODEL), lambda i: (i, 0))
    resident = lambda a: pl.BlockSpec((1,) + a.shape[1:], lambda i: (layer, 0, 0), pipeline_mode=pl.Buffered(1))
    return pl.pallas_call(
        _ffn_kernel,
        grid=(s // TM,),
        in_specs=[row, row, resident(wg), resident(wu), resident(wd)],
        out_specs=row,
        out_shape=jax.ShapeDtypeStruct((s, D_MODEL), F32),
        compiler_params=_cparams(("parallel",)),
        name="ffn_dense",
    )(h2, x1, wg, wu, wd)


def _for_rows(n, fn):
    for r in range(n):
        fn(r)


def _dispatch_kernel(fill_blk, n_used, pos_ref, h_ref, xs_hbm, zbuf, sem, zsem):
    n_blocks = xs_hbm.shape[0] // MOE_BLOCK

    def zero_fill(blk):
        return pltpu.make_async_copy(
            zbuf, xs_hbm.at[pl.ds(pl.multiple_of(blk * MOE_BLOCK, MOE_BLOCK), MOE_BLOCK)], zsem)

    @pl.when(pl.program_id(0) == 0)
    def _():
        zbuf[...] = jnp.zeros_like(zbuf)
        def fill(blk):
            cp = zero_fill(blk)
            cp.start()
            cp.wait()

        for e in range(N_EXPERTS):
            fill(fill_blk[e])

        def tail(blk, c):
            fill(blk)
            return c
        lax.fori_loop(n_used[0], n_blocks, tail, 0)

    def row_copy(r, k):
        return pltpu.make_async_copy(h_ref.at[pl.ds(r, 1)], xs_hbm.at[pl.ds(pos_ref[0, 0, TOP_K * r + k], 1)], sem)

    def start(r):
        for k in range(TOP_K):
            row_copy(r, k).start(priority=k)

    def wait(r):
        for k in range(TOP_K):
            row_copy(r, k).wait()

    _for_rows(TM, start)
    _for_rows(TM, wait)


def _moe_dispatch(h2, pos, fill_blk, n_used, n_blocks):
    s = h2.shape[0]
    grid_spec = pltpu.PrefetchScalarGridSpec(
        num_scalar_prefetch=2,
        grid=(s // TM,),
        in_specs=[pl.BlockSpec((1, 1, TOP_K * TM), lambda i, f, u: (i, 0, 0), memory_space=pltpu.SMEM),
                  pl.BlockSpec((TM, D_MODEL), lambda i, f, u: (i, 0))],
        out_specs=pl.BlockSpec(memory_space=pl.ANY),
        scratch_shapes=[pltpu.VMEM((MOE_BLOCK, D_MODEL), F32), pltpu.SemaphoreType.DMA,
                        pltpu.SemaphoreType.DMA],
    )
    return pl.pallas_call(
        _dispatch_kernel,
        grid_spec=grid_spec,
        out_shape=jax.ShapeDtypeStruct((n_blocks * MOE_BLOCK, D_MODEL), F32),
        compiler_params=_cparams(("arbitrary",)),
        name="moe_dispatch",
    )(fill_blk, n_used, pos.reshape(s // TM, 1, TOP_K * TM), h2)


def _moe_kernel(blk_e, n_used, xs_ref, wg_ref, wu_ref, wd_ref, o_ref):
    b = pl.program_id(0)

    @pl.when(b < n_used[0])
    def _():
        o_ref[...] = _swiglu_chunks(xs_ref[...].astype(BF16), wg_ref.at[0, 0], wu_ref.at[0, 0], wd_ref.at[0, 0])

    @pl.when(b >= n_used[0])
    def _():
        o_ref[...] = jnp.zeros_like(o_ref)


def _moe_experts(xs, blk_e, n_used, wg, wu, wd, layer, n_blocks):
    weights = lambda a: pl.BlockSpec((1, 1) + a.shape[2:], lambda b, e, u: (layer, e[b], 0, 0),
                                     pipeline_mode=pl.Buffered(1))
    grid_spec = pltpu.PrefetchScalarGridSpec(
        num_scalar_prefetch=2,
        grid=(n_blocks,),
        in_specs=[pl.BlockSpec((MOE_BLOCK, D_MODEL), lambda b, e, u: (jnp.minimum(b, u[0] - 1), 0)),
                  weights(wg), weights(wu), weights(wd)],
        out_specs=pl.BlockSpec((MOE_BLOCK, D_MODEL), lambda b, e, u: (b, 0)),
    )
    return pl.pallas_call(
        _moe_kernel,
        grid_spec=grid_spec,
        out_shape=jax.ShapeDtypeStruct((n_blocks * MOE_BLOCK, D_MODEL), F32),
        compiler_params=_cparams(("arbitrary",)),
        name="moe_experts",
    )(blk_e, n_used, xs, wg, wu, wd)


def _combine_kernel(pos_ref, x_ref, gate_ref, ys_hbm, o_ref, ybuf, sem):
    def row_copy(r, k):
        return pltpu.make_async_copy(ys_hbm.at[pl.ds(pos_ref[0, 0, TOP_K * r + k], 1)],
                                     ybuf.at[k, pl.ds(r, 1)], sem)

    def start(r):
        for k in range(TOP_K):
            row_copy(r, k).start(priority=k)

    def wait(r):
        for k in range(TOP_K):
            row_copy(r, k).wait()

    _for_rows(TM, start)
    _for_rows(TM, wait)
    gates = gate_ref[...]
    gated = [ybuf[k] * gates[:, k:k + 1] for k in range(TOP_K)]
    o_ref[...] = x_ref[...] + functools.reduce(lambda a, b_: a + b_, gated)


def _moe_combine(x1, gates, pos, ys):
    s = x1.shape[0]
    row = lambda w_: pl.BlockSpec((TM, w_), lambda i: (i, 0))
    return pl.pallas_call(
        _combine_kernel,
        grid=(s // TM,),
        in_specs=[pl.BlockSpec((1, 1, TOP_K * TM), lambda i: (i, 0, 0), memory_space=pltpu.SMEM),
                  row(D_MODEL), row(TOP_K), pl.BlockSpec(memory_space=pl.ANY)],
        out_specs=row(D_MODEL),
        out_shape=jax.ShapeDtypeStruct((s, D_MODEL), F32),
        scratch_shapes=[pltpu.VMEM((TOP_K, TM, D_MODEL), F32), pltpu.SemaphoreType.DMA],
        compiler_params=_cparams(("arbitrary",)),
        name="moe_combine",
    )(pos.reshape(s // TM, 1, TOP_K * TM), x1, gates, ys)


def _moe_layer(x1, h2, route, wg, wu, wd, layer):
    s = h2.shape[0]
    n_blocks = s * TOP_K // MOE_BLOCK + N_EXPERTS
    flat_e = route[:TOP_K].T.astype(jnp.int32).reshape(-1)
    gates = route[TOP_K:2 * TOP_K].T
    onehot = (flat_e[:, None] == jnp.arange(N_EXPERTS)[None, :]).astype(jnp.int32)
    csum = jnp.cumsum(onehot, axis=0)
    counts = csum[-1]
    padded = (counts + MOE_BLOCK - 1) // MOE_BLOCK * MOE_BLOCK
    pend = jnp.cumsum(padded)
    pstart = pend - padded
    pos = jnp.sum(onehot * (csum - 1 + pstart[None, :]), axis=1).astype(jnp.int32)
    blk_start = jnp.arange(n_blocks, dtype=jnp.int32) * MOE_BLOCK
    blk_e = jnp.minimum(jnp.sum((pend[None, :] <= blk_start[:, None]).astype(jnp.int32), axis=1),
                        N_EXPERTS - 1).astype(jnp.int32)
    n_used = (pend[-1] // MOE_BLOCK).astype(jnp.int32).reshape(1)
    fill_blk = jnp.maximum(pend // MOE_BLOCK - 1, 0).astype(jnp.int32)
    xs = _moe_dispatch(h2, pos, fill_blk, n_used, n_blocks)
    ys = _moe_experts(xs, blk_e, n_used, wg, wu, wd, layer, n_blocks)
    return _moe_combine(x1, gates, pos, ys)


def _block_diag_ones(width, group):
    idx = jnp.arange(width) // group
    return (idx[:, None] == idx[None, :]).astype(BF16)


def kernel(x, norm_mix, w_in, q_norm_sb, k_norm_sb, q_norm_ca, k_norm_ca, rel_bias, lam_re, lam_im, log_dt, b_re, b_im, c_re, c_im, d_skip, w_glu, g_out, w_out, norm_ffn, w_gate_dense, w_up_dense, w_down_dense, w_router, w_gate_moe, w_up_moe, w_down_moe):
    b, s, d = x.shape
    assert b == 1 and d == D_MODEL and s % (TM * 2) == 0
    xs = x.reshape(s, d).astype(F32)
    scale = 1.0 / math.sqrt(HEAD_DIM)
    gm = _block_diag_ones(GROUP_SUM_WIDTH, HEAD_DIM)
    idx = jnp.arange(SB_BLOCK)
    tri = (idx[:, None] > idx[None, :]).astype(BF16)
    vec = lambda a: a.reshape(1, -1).astype(F32)
    tile = lambda a, width: jnp.tile(a.astype(F32), width // HEAD_DIM).reshape(1, width)

    dense_w = [w.astype(BF16) for w in (w_gate_dense, w_up_dense, w_down_dense)]
    moe_w = [w.astype(F32) for w in (w_gate_moe, w_up_moe, w_down_moe)]
    ca_bias = _ca_bias_mask(rel_bias.reshape(DEPTH * CA_HEADS, 2 * REL_CLIP + 1))

    for l in range(DEPTH):
        q_sb, k_sb, v_sb, q_ca, k_ca, v_ca, u = _in_proj(
            xs, vec(norm_mix[l]), w_in[l].astype(BF16),
            tile(q_norm_sb[l], SB_WIDTH) * (scale * LOG2E), tile(k_norm_sb[l], SB_WIDTH),
            tile(q_norm_ca[l], CA_WIDTH) * scale, tile(k_norm_ca[l], CA_WIDTH), gm)
        o_sb = _sb_attention(q_sb, k_sb, v_sb, tri)
        o_ca = _ca_attention(q_ca, k_ca, v_ca, ca_bias, layer=l)
        o_ssm = _s5_mixer(u, lam_re[l], lam_im[l], log_dt[l], b_re[l], b_im[l], c_re[l], c_im[l],
                          d_skip[l], w_glu[l])
        i = l // 2
        if l % 2 == 0:
            x1, h2 = _mix_out(o_sb, o_ca, o_ssm, xs, vec(g_out[l]), w_out[l].astype(BF16), vec(norm_ffn[l]))
            xs = _ffn_dense(h2, x1, *dense_w, layer=i)
        else:
            r_hi, r_lo = _split_bf16(w_router[i].astype(F32).T)
            r_both = jnp.concatenate([r_hi, r_lo], axis=0)
            r_hi = jnp.concatenate([r_hi, jnp.zeros_like(r_hi)], axis=0)
            x1, h2, route = _mix_out(o_sb, o_ca, o_ssm, xs, vec(g_out[l]), w_out[l].astype(BF16),
                                     vec(norm_ffn[l]), router=(r_both, r_hi))
            xs = _moe_layer(x1, h2, route, *moe_w, layer=i)
    return xs.reshape(b, s, d).astype(x.dtype)
```

```python
import functools
import math

import jax
import jax.numpy as jnp
import numpy as np
from jax import lax
from jax.experimental import pallas as pl
from jax.experimental.pallas import tpu as pltpu

F32 = jnp.float32
BF16 = jnp.bfloat16
MIX_DTYPE = BF16

D_MODEL = 1024
DEPTH = 4
CHUNK = 64
HEAD_DIM = 64
SB_WIDTH = 256
CA_WIDTH = 512
CA_HEADS = 8
SSM_WIDTH = 256
SSM_GROUP_CH = 16
SSM_GROUPS = 16
SSM_STATE = 64
N_STATES = SSM_GROUPS * SSM_STATE
D_IN_PROJ = 3 * SB_WIDTH + 3 * CA_WIDTH + SSM_WIDTH
MASK_LOGIT = -1e4
LOOKBACK = 8
REL_CLIP = 128
D_FF = 2816
N_EXPERTS = 8
TOP_K = 2
RMS_EPS = 1e-6

LANES = 128
SUBLANES = 8
VMEM_LIMIT = 56 * 1024 * 1024

TM = 512
SB_BLOCK = 256
SB_BLOCKS_PER_ITER = 2
CA_BLOCK = 256
MOE_BLOCK = 512
SB_UNDERFLOW = 104.0
LOG2E = math.log2(math.e)


def _cparams(sem):
    return pltpu.CompilerParams(dimension_semantics=sem, vmem_limit_bytes=VMEM_LIMIT)


def _split_bf16(x):
    hi = x.astype(BF16)
    lo = (x - hi.astype(F32)).astype(BF16)
    return hi, lo


def _rms(x, gain):
    ms = jnp.mean(x * x, axis=-1, keepdims=True)
    return x * lax.rsqrt(ms + RMS_EPS) * gain


GROUP_SUM_WIDTH = 256


def _head_rms(p, gmat, gain):
    sq = (p * p).astype(BF16)
    blocks = []
    for c in range(p.shape[1] // GROUP_SUM_WIDTH):
        cols = slice(c * GROUP_SUM_WIDTH, (c + 1) * GROUP_SUM_WIDTH)
        blocks.append(jnp.dot(sq[:, cols], gmat, preferred_element_type=F32))
    ss = blocks[0] if len(blocks) == 1 else jnp.concatenate(blocks, axis=1)
    return p * lax.rsqrt(ss * (1.0 / HEAD_DIM) + RMS_EPS) * gain


def _in_proj_kernel(x_ref, g_ref, w_ref, gq_sb, gk_sb, gq_ca, gk_ca, gm_ref,
                    qsb_o, ksb_o, vsb_o, qca_o, kca_o, vca_o, u_o):
    h = _rms(x_ref[...], g_ref[...])
    proj = jnp.dot(h.astype(BF16), w_ref[...], preferred_element_type=F32)
    gm = gm_ref[...]
    o1, o2, o3 = SB_WIDTH, 2 * SB_WIDTH, 3 * SB_WIDTH
    o4, o5, o6 = o3 + CA_WIDTH, o3 + 2 * CA_WIDTH, o3 + 3 * CA_WIDTH
    qsb_o[...] = _head_rms(proj[:, :o1], gm, gq_sb[...]).astype(BF16)
    ksb_o[...] = _head_rms(proj[:, o1:o2], gm, gk_sb[...]).astype(BF16)
    vsb_o[...] = proj[:, o2:o3].astype(BF16)
    qca_o[...] = _head_rms(proj[:, o3:o4], gm, gq_ca[...]).astype(BF16)
    kca_o[...] = _head_rms(proj[:, o4:o5], gm, gk_ca[...]).astype(BF16)
    vca_o[...] = proj[:, o5:o6].astype(BF16)
    u_o[...] = proj[:, o6:]


def _in_proj(x, g, w, gq_sb, gk_sb, gq_ca, gk_ca, gm):
    s = x.shape[0]
    row = lambda w_: pl.BlockSpec((TM, w_), lambda i: (i, 0))
    full = lambda a: pl.BlockSpec(a.shape, lambda i: (0,) * a.ndim)
    return pl.pallas_call(
        _in_proj_kernel,
        grid=(s // TM,),
        in_specs=[row(D_MODEL)] + [full(a) for a in (g, w, gq_sb, gk_sb, gq_ca, gk_ca, gm)],
        out_specs=[row(SB_WIDTH)] * 3 + [row(CA_WIDTH)] * 3 + [row(SSM_WIDTH)],
        out_shape=[jax.ShapeDtypeStruct((s, SB_WIDTH), BF16)] * 3
        + [jax.ShapeDtypeStruct((s, CA_WIDTH), BF16)] * 3
        + [jax.ShapeDtypeStruct((s, SSM_WIDTH), F32)],
        compiler_params=_cparams(("parallel",)),
        name="in_proj",
    )(x, g, w, gq_sb, gk_sb, gq_ca, gk_ca, gm)


def _sb_kernel(q_ref, k_ref, v_ref, tri_ref, o_ref):
    i = pl.program_id(0)
    tri = tri_ref[...]
    row = lax.broadcasted_iota(jnp.int32, (SB_BLOCK, SB_BLOCK), 0)
    col = lax.broadcasted_iota(jnp.int32, (SB_BLOCK, SB_BLOCK), 1)
    rel = col - row
    lane = lax.broadcasted_iota(jnp.int32, (SB_BLOCK, LANES), 1)
    n_pairs = SB_WIDTH // LANES
    per_pair = LANES // HEAD_DIM
    in_head = [(lane >= hh * HEAD_DIM) & (lane < (hh + 1) * HEAD_DIM) for hh in range(per_pair)]
    qm = [jnp.where(in_head[hh], q_ref[:, p * LANES:(p + 1) * LANES].astype(F32), 0.0).astype(BF16)
          for p in range(n_pairs) for hh in range(per_pair)]

    def cond(c):
        j, done = c[0], c[1]
        return jnp.logical_and(j >= 0, jnp.logical_not(done))

    def body(c):
        j, _, carries, accs = c
        heads = range(n_pairs * per_pair)
        lanes = [slice(p * LANES, (p + 1) * LANES) for p in range(n_pairs)]
        add = lambda a, b_: a + b_
        blocks = range(SB_BLOCKS_PER_ITER)
        starts = [pl.multiple_of(jnp.maximum(j - b_, 0) * SB_BLOCK, SB_BLOCK) for b_ in blocks]
        visible = [jnp.logical_and(rel < (i - j + b_) * SB_BLOCK, j - b_ >= 0) for b_ in blocks]
        kb = [[k_ref[pl.ds(starts[b_], SB_BLOCK), lanes[p]] for p in range(n_pairs)] for b_ in blocks]
        vb = [[v_ref[pl.ds(starts[b_], SB_BLOCK), lanes[p]] for p in range(n_pairs)] for b_ in blocks]
        tiles = [(b_, h) for b_ in blocks for h in heads]
        z = {t: lax.dot_general(qm[t[1]], kb[t[0]][t[1] // per_pair], (((1,), (1,)), ((), ())),
                                preferred_element_type=F32) for t in tiles}
        z = {t: jnp.where(visible[t[0]], z[t], MASK_LOGIT * LOG2E) for t in tiles}
        sp = {t: jnp.maximum(z[t], 0.0) + jnp.log2(1.0 + jnp.exp2(-jnp.abs(z[t]))) for t in tiles}
        later = {t: jnp.dot(sp[t].astype(BF16), tri, preferred_element_type=F32) for t in tiles}
        total = {t: jnp.sum(sp[t], axis=-1, keepdims=True) for t in tiles}
        carry = {}
        for h in heads:
            running = carries[h]
            for b_ in blocks:
                carry[b_, h] = running
                running = running + total[b_, h]
            carry[SB_BLOCKS_PER_ITER, h] = running
        w = {t: jnp.exp2(z[t] - sp[t] - (later[t] + carry[t])).astype(BF16) for t in tiles}
        pv = {t: jnp.dot(w[t], vb[t[0]][t[1] // per_pair], preferred_element_type=F32) for t in tiles}
        new_accs = [accs[p] + functools.reduce(
            add, [jnp.where(in_head[hh], functools.reduce(add, [pv[b_, p * per_pair + hh] for b_ in blocks]), 0.0)
                  for hh in range(per_pair)])
            for p in range(n_pairs)]
        new_carries = [carry[SB_BLOCKS_PER_ITER, h] for h in heads]
        done = jnp.min(functools.reduce(jnp.minimum, new_carries)) > SB_UNDERFLOW * LOG2E
        return j - SB_BLOCKS_PER_ITER, done, tuple(new_carries), tuple(new_accs)

    init = (i, False,
            tuple(jnp.zeros((SB_BLOCK, 1), F32) for _ in range(n_pairs * per_pair)),
            tuple(jnp.zeros((SB_BLOCK, LANES), F32) for _ in range(n_pairs)))
    accs = lax.while_loop(cond, body, init)[3]
    for p in range(n_pairs):
        o_ref[:, p * LANES:(p + 1) * LANES] = accs[p].astype(o_ref.dtype)


def _sb_attention(q, k, v, tri):
    s = q.shape[0]
    whole = pl.BlockSpec(memory_space=pltpu.VMEM)
    return pl.pallas_call(
        _sb_kernel,
        grid=(s // SB_BLOCK,),
        in_specs=[pl.BlockSpec((SB_BLOCK, SB_WIDTH), lambda i: (i, 0)), whole, whole, whole],
        out_specs=pl.BlockSpec((SB_BLOCK, SB_WIDTH), lambda i: (i, 0)),
        out_shape=jax.ShapeDtypeStruct((s, SB_WIDTH), MIX_DTYPE),
        compiler_params=_cparams(("parallel",)),
        name="sb_attention",
    )(q, k, v, tri)


CA_WINDOW = 3
CA_PAIRS_PER_GROUP = 2


def _ca_kernel(q_ref, k0, k1, k2, v0, v1, v2, bm_ref, o_ref):
    i = pl.program_id(0)
    k_refs = (k0, k1, k2)
    v_refs = (v0, v1, v2)
    lane = lax.broadcasted_iota(jnp.int32, (CA_BLOCK, LANES), 1)
    per_pair = LANES // HEAD_DIM
    in_head = [(lane >= hh * HEAD_DIM) & (lane < (hh + 1) * HEAD_DIM) for hh in range(per_pair)]
    add = lambda a, b_: a + b_
    tiles = range(CA_WINDOW)
    for group in range(CA_WIDTH // LANES // CA_PAIRS_PER_GROUP):
        pairs = [group * CA_PAIRS_PER_GROUP + g for g in range(CA_PAIRS_PER_GROUP)]
        heads = [(p, hh) for p in pairs for hh in range(per_pair)]
        lanes = {p: slice(p * LANES, (p + 1) * LANES) for p in pairs}
        qm = {(p, hh): jnp.where(in_head[hh], q_ref[:, lanes[p]].astype(F32), 0.0).astype(BF16)
              for p, hh in heads}
        sc = {(hd, jj): lax.dot_general(qm[hd], k_refs[jj][:, lanes[hd[0]]], (((1,), (1,)), ((), ())),
                                        preferred_element_type=F32)
              for hd in heads for jj in tiles}
        sc = {(hd, jj): jnp.where(i + jj >= CA_WINDOW - 1,
                                  sc[hd, jj] + bm_ref[hd[0] * per_pair + hd[1], :, jj * CA_BLOCK:(jj + 1) * CA_BLOCK],
                                  -1e30)
              for hd in heads for jj in tiles}
        m = {hd: functools.reduce(jnp.maximum, [jnp.max(sc[hd, jj], axis=-1, keepdims=True) for jj in tiles])
             for hd in heads}
        p_un = {(hd, jj): jnp.exp(sc[hd, jj] - m[hd]) for hd in heads for jj in tiles}
        denom = {hd: functools.reduce(add, [jnp.sum(p_un[hd, jj], axis=-1, keepdims=True) for jj in tiles])
                 for hd in heads}
        pv = {hd: functools.reduce(add, [jnp.dot(p_un[hd, jj].astype(BF16), v_refs[jj][:, lanes[hd[0]]],
                                                 preferred_element_type=F32) for jj in tiles])
              for hd in heads}
        for p in pairs:
            o_ref[:, lanes[p]] = functools.reduce(
                add, [jnp.where(in_head[hh], pv[p, hh] / denom[p, hh], 0.0) for hh in range(per_pair)]
            ).astype(o_ref.dtype)


def _ca_attention(q, k, v, bm, layer):
    s = q.shape[0]
    blk = lambda off: pl.BlockSpec((CA_BLOCK, CA_WIDTH), lambda i: (jnp.maximum(i - off, 0), 0))
    return pl.pallas_call(
        _ca_kernel,
        grid=(s // CA_BLOCK,),
        in_specs=[blk(0), blk(2), blk(1), blk(0), blk(2), blk(1), blk(0),
                  pl.BlockSpec((CA_HEADS,) + bm.shape[1:], lambda i: (layer, 0, 0))],
        out_specs=blk(0),
        out_shape=jax.ShapeDtypeStruct((s, CA_WIDTH), MIX_DTYPE),
        compiler_params=_cparams(("parallel",)),
        name="ca_attention",
    )(q, k, k, k, v, v, v, bm)


CA_BIAS_SPAN = 1024


def _ca_bias_kernel(g_ref, o_ref):
    width = CA_WINDOW * CA_BLOCK
    rows = jnp.broadcast_to(g_ref[0], (CA_BLOCK, CA_BIAS_SPAN))
    toeplitz = pltpu.roll(rows, 0, 1, stride=1, stride_axis=0)[:, :width]
    r = lax.broadcasted_iota(jnp.int32, (CA_BLOCK, width), 0)
    c = lax.broadcasted_iota(jnp.int32, (CA_BLOCK, width), 1)
    back = r // CHUNK - (c // CHUNK - LOOKBACK)
    o_ref[0] = jnp.where((back >= 0) & (back <= LOOKBACK), toeplitz, -1e30)


def _ca_bias_mask(rel_bias):
    planes = rel_bias.shape[0]
    width = CA_WINDOW * CA_BLOCK
    assert CA_BLOCK + width - 1 <= CA_BIAS_SPAN
    m = np.arange(CA_BIAS_SPAN)
    offset = np.where(m < width, m, m - CA_BIAS_SPAN)
    dist = LOOKBACK * CHUNK - offset
    by_offset = rel_bias.astype(F32)[:, np.clip(dist, -REL_CLIP, REL_CLIP) + REL_CLIP]
    return pl.pallas_call(
        _ca_bias_kernel,
        grid=(planes,),
        in_specs=[pl.BlockSpec((1, 1, CA_BIAS_SPAN), lambda i: (i, 0, 0))],
        out_specs=pl.BlockSpec((1, CA_BLOCK, width), lambda i: (i, 0, 0)),
        out_shape=jax.ShapeDtypeStruct((planes, CA_BLOCK, width), F32),
        compiler_params=_cparams(("parallel",)),
        name="ca_bias",
    )(by_offset.reshape(planes, 1, CA_BIAS_SPAN))


def _s5_disc_kernel(lre_ref, lim_ref, ldt_ref, bre_ref, bim_ref, are_o, aim_o, bbre_o, bbim_o):
    lre = lre_ref[...]
    lim = lim_ref[...]
    dt = jnp.exp(ldt_ref[...])
    mag = jnp.exp(lre * dt)
    ang = lim * dt
    a_re = mag * jnp.cos(ang)
    a_im = mag * jnp.sin(ang)
    are_o[...] = a_re
    aim_o[...] = a_im
    nre = a_re - 1.0
    nim = a_im
    den = lre * lre + lim * lim
    zoh_re = (nre * lre + nim * lim) / den
    zoh_im = (nim * lre - nre * lim) / den
    bre = bre_ref[...]
    bim = bim_ref[...]
    bbre_o[...] = zoh_re * bre - zoh_im * bim
    bbim_o[...] = zoh_re * bim + zoh_im * bre


def _s5_discretize(lam_re, lam_im, log_dt, b_re, b_im):
    col = lambda a: a.reshape(N_STATES, 1).astype(F32)
    ldt = jnp.repeat(log_dt.astype(F32), SSM_STATE).reshape(N_STATES, 1)
    b2 = lambda a: a.reshape(N_STATES, SSM_GROUP_CH).astype(F32)
    return pl.pallas_call(
        _s5_disc_kernel,
        out_shape=[jax.ShapeDtypeStruct((N_STATES, 1), F32)] * 2
        + [jax.ShapeDtypeStruct((N_STATES, SSM_GROUP_CH), F32)] * 2,
        name="s5_discretize",
    )(col(lam_re), col(lam_im), ldt, b2(b_re), b2(b_im))


S5_SEGS = SUBLANES
S5_STEPS = 128


def _cmul(a_re, a_im, b_re, b_im):
    return a_re * b_re - a_im * b_im, a_re * b_im + a_im * b_re


def _s5_local_kernel(u_ref, wre_ref, wim_ref, are_ref, aim_ref, cre_ref, cim_ref, y_o, end_o,
                     xre, xim, st):
    @pl.when(pl.program_id(0) == 0)
    def _():
        st[...] = jnp.zeros_like(st)

    u = u_ref[...].astype(BF16)
    xre[...] = jnp.dot(u, wre_ref[...], preferred_element_type=F32)
    xim[...] = jnp.dot(u, wim_ref[...], preferred_element_type=F32)
    a_re = jnp.broadcast_to(are_ref[...], (S5_SEGS, N_STATES))
    a_im = jnp.broadcast_to(aim_ref[...], (S5_SEGS, N_STATES))
    x_re, x_im = st[0], st[1]
    for k in range(S5_STEPS):
        rows = slice(k * S5_SEGS, (k + 1) * S5_SEGS)
        p_re, p_im = _cmul(a_re, a_im, x_re, x_im)
        x_re = p_re + xre[rows, :]
        x_im = p_im + xim[rows, :]
        xre[rows, :] = x_re
        xim[rows, :] = x_im
    st[0] = x_re
    st[1] = x_im
    end_o[0] = x_re
    end_o[1] = x_im
    y_o[...] = (jnp.dot(xre[...].astype(BF16), cre_ref[...], preferred_element_type=F32)
                - jnp.dot(xim[...].astype(BF16), cim_ref[...], preferred_element_type=F32))


def _s5_finish_kernel(seg_len, y_ref, u_ref, end_ref, are_ref, aim_ref, cre_ref, cim_ref, d_ref, wglu_ref,
                      o_ref, zre, zim, x0, pw):
    a_re = jnp.broadcast_to(are_ref[...], (S5_SEGS, N_STATES))
    a_im = jnp.broadcast_to(aim_ref[...], (S5_SEGS, N_STATES))

    @pl.when(pl.program_id(0) == 0)
    def _():
        r_re, r_im = are_ref[...], aim_ref[...]
        p_re, p_im = jnp.ones_like(r_re), jnp.zeros_like(r_re)
        n = seg_len
        while n:
            if n & 1:
                p_re, p_im = _cmul(p_re, p_im, r_re, r_im)
            r_re, r_im = _cmul(r_re, r_im, r_re, r_im)
            n >>= 1
        rows_re, rows_im = [jnp.zeros_like(p_re)], [jnp.zeros_like(p_re)]
        for s in range(1, S5_SEGS):
            c_re, c_im = _cmul(p_re, p_im, rows_re[-1], rows_im[-1])
            rows_re.append(c_re + end_ref[0, s - 1:s, :])
            rows_im.append(c_im + end_ref[1, s - 1:s, :])
        x0[0] = jnp.concatenate(rows_re, axis=0)
        x0[1] = jnp.concatenate(rows_im, axis=0)
        pw[0] = a_re
        pw[1] = a_im

    x0_re = x0[0]
    x0_im = x0[1]
    p_re, p_im = pw[0], pw[1]
    for k in range(S5_STEPS):
        rows = slice(k * S5_SEGS, (k + 1) * S5_SEGS)
        zre[rows, :], zim[rows, :] = _cmul(p_re, p_im, x0_re, x0_im)
        p_re, p_im = _cmul(a_re, a_im, p_re, p_im)
    pw[0] = p_re
    pw[1] = p_im
    y = y_ref[...] + (jnp.dot(zre[...].astype(BF16), cre_ref[...], preferred_element_type=F32)
                      - jnp.dot(zim[...].astype(BF16), cim_ref[...], preferred_element_type=F32))
    y = y + d_ref[...] * u_ref[...]
    y = 0.5 * y * (1.0 + jnp.tanh(math.sqrt(2.0 / math.pi) * (y + 0.044715 * (y * y * y))))
    hg = jnp.dot(y.astype(BF16), wglu_ref[...], preferred_element_type=F32)
    o_ref[...] = (hg[:, :SSM_WIDTH] / (1.0 + jnp.exp(-hg[:, SSM_WIDTH:]))).astype(o_ref.dtype)


def _s5_mixer(u, lam_re, lam_im, log_dt, b_re, b_im, c_re, c_im, d_skip, w_glu):
    s = u.shape[0]
    seg_len = s // S5_SEGS
    rows = S5_STEPS * S5_SEGS
    a_re, a_im, bb_re, bb_im = _s5_discretize(lam_re, lam_im, log_dt, b_re, b_im)
    a_re = a_re.reshape(1, N_STATES)
    a_im = a_im.reshape(1, N_STATES)
    eye = jnp.eye(SSM_GROUPS, dtype=F32)

    def in_matrix(bb):
        bb = bb.reshape(SSM_GROUPS, SSM_STATE, SSM_GROUP_CH)
        return jnp.einsum('gpc,gh->gchp', bb, eye).reshape(SSM_WIDTH, N_STATES).astype(BF16)

    def out_matrix(c):
        return jnp.einsum('gcp,gh->gphc', c.astype(F32), eye).reshape(N_STATES, SSM_WIDTH).astype(BF16)

    w_re, w_im = in_matrix(bb_re), in_matrix(bb_im)
    cm_re, cm_im = out_matrix(c_re), out_matrix(c_im)
    d = d_skip.reshape(1, SSM_WIDTH).astype(F32)
    wg = w_glu.astype(BF16)
    u_seg = u.reshape(S5_SEGS, seg_len, SSM_WIDTH).transpose(1, 0, 2).reshape(s, SSM_WIDTH)
    full = lambda a: pl.BlockSpec(a.shape, lambda i: (0,) * a.ndim)
    row = pl.BlockSpec((rows, SSM_WIDTH), lambda i: (i, 0))
    ends = pl.BlockSpec((2, S5_SEGS, N_STATES), lambda i: (0, 0, 0))
    big = pltpu.VMEM((rows, N_STATES), F32)
    small = pltpu.VMEM((2, S5_SEGS, N_STATES), F32)
    y_local, end = pl.pallas_call(
        _s5_local_kernel,
        grid=(seg_len // S5_STEPS,),
        in_specs=[row] + [full(a) for a in (w_re, w_im, a_re, a_im, cm_re, cm_im)],
        out_specs=[row, ends],
        out_shape=[jax.ShapeDtypeStruct((s, SSM_WIDTH), F32),
                   jax.ShapeDtypeStruct((2, S5_SEGS, N_STATES), F32)],
        scratch_shapes=[big, big, small],
        compiler_params=_cparams(("arbitrary",)),
        name="s5_local",
    )(u_seg, w_re, w_im, a_re, a_im, cm_re, cm_im)
    o_seg = pl.pallas_call(
        functools.partial(_s5_finish_kernel, seg_len),
        grid=(seg_len // S5_STEPS,),
        in_specs=[row, row, ends] + [full(a) for a in (a_re, a_im, cm_re, cm_im, d, wg)],
        out_specs=row,
        out_shape=jax.ShapeDtypeStruct((s, SSM_WIDTH), MIX_DTYPE),
        scratch_shapes=[big, big, small, small],
        compiler_params=_cparams(("arbitrary",)),
        name="s5_finish",
    )(y_local, u_seg, end, a_re, a_im, cm_re, cm_im, d, wg)
    return o_seg.reshape(seg_len, S5_SEGS, SSM_WIDTH).transpose(1, 0, 2).reshape(s, SSM_WIDTH)


def _mix_core(osb_ref, oca_ref, ossm_ref, x_ref, g_ref, w_ref, nf_ref):
    g = g_ref[...]
    mixed = jnp.concatenate([
        _rms(osb_ref[...].astype(F32), g[:, :SB_WIDTH]),
        _rms(oca_ref[...].astype(F32), g[:, SB_WIDTH:SB_WIDTH + CA_WIDTH]),
        _rms(ossm_ref[...].astype(F32), g[:, SB_WIDTH + CA_WIDTH:]),
    ], axis=-1)
    x1 = x_ref[...] + jnp.dot(mixed.astype(BF16), w_ref[...], preferred_element_type=F32)
    return x1, _rms(x1, nf_ref[...])


def _mix_dense_kernel(osb_ref, oca_ref, ossm_ref, x_ref, g_ref, w_ref, nf_ref, x1_o, h2_o):
    x1, h2 = _mix_core(osb_ref, oca_ref, ossm_ref, x_ref, g_ref, w_ref, nf_ref)
    x1_o[...] = x1
    h2_o[...] = h2.astype(BF16)


def _mix_moe_kernel(osb_ref, oca_ref, ossm_ref, x_ref, g_ref, w_ref, nf_ref, rboth_ref, rhi_ref,
                    x1_o, h2_o, route_o):
    x1, h2 = _mix_core(osb_ref, oca_ref, ossm_ref, x_ref, g_ref, w_ref, nf_ref)
    x1_o[...] = x1
    h2_o[...] = h2
    hi, lo = _split_bf16(h2)
    nt = (((1,), (1,)), ((), ()))
    both = lax.dot_general(rboth_ref[...], hi, nt, preferred_element_type=F32)
    cross = lax.dot_general(rhi_ref[...], lo, nt, preferred_element_type=F32)
    lg = both[:N_EXPERTS] + both[N_EXPERTS:] + cross[:N_EXPERTS]
    idx = lax.broadcasted_iota(jnp.int32, lg.shape, 0)
    neg = -jnp.inf
    m1 = jnp.max(lg, axis=0, keepdims=True)
    i1 = jnp.min(jnp.where(lg == m1, idx, N_EXPERTS), axis=0, keepdims=True)
    lg2 = jnp.where(idx == i1, neg, lg)
    m2 = jnp.max(lg2, axis=0, keepdims=True)
    i2 = jnp.min(jnp.where(lg2 == m2, idx, N_EXPERTS), axis=0, keepdims=True)
    e = jnp.exp(m2 - m1)
    g1 = 1.0 / (1.0 + e)
    g2 = e / (1.0 + e)
    route_o[...] = jnp.where(idx == 0, i1.astype(F32),
                             jnp.where(idx == 1, i2.astype(F32),
                                       jnp.where(idx == 2, g1, jnp.where(idx == 3, g2, 0.0))))


def _mix_out(o_sb, o_ca, o_ssm, x, g, w, nf, router=None):
    s = x.shape[0]
    row = lambda w_: pl.BlockSpec((TM, w_), lambda i: (i, 0))
    full = lambda a: pl.BlockSpec(a.shape, lambda i: (0, 0))
    ins = [o_sb, o_ca, o_ssm, x, g, w, nf]
    in_specs = [row(SB_WIDTH), row(CA_WIDTH), row(SSM_WIDTH), row(D_MODEL), full(g), full(w), full(nf)]
    if router is None:
        return pl.pallas_call(
            _mix_dense_kernel, grid=(s // TM,), in_specs=in_specs,
            out_specs=[row(D_MODEL)] * 2,
            out_shape=[jax.ShapeDtypeStruct((s, D_MODEL), F32), jax.ShapeDtypeStruct((s, D_MODEL), BF16)],
            compiler_params=_cparams(("parallel",)), name="mix_out_dense",
        )(*ins)
    return pl.pallas_call(
        _mix_moe_kernel, grid=(s // TM,), in_specs=in_specs + [full(router[0]), full(router[1])],
        out_specs=[row(D_MODEL), row(D_MODEL), pl.BlockSpec((N_EXPERTS, TM), lambda i: (0, i))],
        out_shape=[jax.ShapeDtypeStruct((s, D_MODEL), F32)] * 2 + [jax.ShapeDtypeStruct((N_EXPERTS, s), F32)],
        compiler_params=_cparams(("parallel",)), name="mix_out_moe",
    )(*ins, *router)


def _swiglu_tile(xb, wg, wu, wd):
    hg = jnp.dot(xb, wg, preferred_element_type=F32)
    hu = jnp.dot(xb, wu, preferred_element_type=F32)
    hid = hg / (1.0 + jnp.exp(-hg)) * hu
    return jnp.dot(hid.astype(BF16), wd, preferred_element_type=F32)


FF_CHUNKS = (1024, 1024, 768)


def _swiglu_chunks(xb, wg_ref, wu_ref, wd_ref):
    assert sum(FF_CHUNKS) == D_FF
    bf16 = lambda a: a if a.dtype == BF16 else a.astype(BF16)
    y = None
    start = 0
    for width in FF_CHUNKS:
        cols = slice(start, start + width)
        part = _swiglu_tile(xb, bf16(wg_ref[:, cols]), bf16(wu_ref[:, cols]), bf16(wd_ref[cols, :]))
        y = part if y is None else y + part
        start += width
    return y


def _ffn_kernel(h_ref, x_ref, wg_ref, wu_ref, wd_ref, o_ref):
    o_ref[...] = x_ref[...] + _swiglu_chunks(h_ref[...], wg_ref.at[0], wu_ref.at[0], wd_ref.at[0])


def _ffn_dense(h2, x1, wg, wu, wd, layer):
    s = x1.shape[0]
    row = pl.BlockSpec((TM, D_MODEL), lambda i: (i, 0))
    resident = lambda a: pl.BlockSpec((1,) + a.shape[1:], lambda i: (layer, 0, 0), pipeline_mode=pl.Buffered(1))
    return pl.pallas_call(
        _ffn_kernel,
        grid=(s // TM,),
        in_specs=[row, row, resident(wg), resident(wu), resident(wd)],
        out_specs=row,
        out_shape=jax.ShapeDtypeStruct((s, D_MODEL), F32),
        compiler_params=_cparams(("parallel",)),
        name="ffn_dense",
    )(h2, x1, wg, wu, wd)


def _for_rows(n, fn):
    for r in range(n):
        fn(r)


def _dispatch_kernel(fill_blk, n_used, pos_ref, h_ref, xs_hbm, zbuf, sem, zsem):
    n_blocks = xs_hbm.shape[0] // MOE_BLOCK

    def zero_fill(blk):
        return pltpu.make_async_copy(
            zbuf, xs_hbm.at[pl.ds(pl.multiple_of(blk * MOE_BLOCK, MOE_BLOCK), MOE_BLOCK)], zsem)

    @pl.when(pl.program_id(0) == 0)
    def _():
        zbuf[...] = jnp.zeros_like(zbuf)
        def fill(blk):
            cp = zero_fill(blk)
            cp.start()
            cp.wait()

        for e in range(N_EXPERTS):
            fill(fill_blk[e])

        def tail(blk, c):
            fill(blk)
            return c
        lax.fori_loop(n_used[0], n_blocks, tail, 0)

    def row_copy(r, k):
        return pltpu.make_async_copy(h_ref.at[pl.ds(r, 1)], xs_hbm.at[pl.ds(pos_ref[0, 0, TOP_K * r + k], 1)], sem)

    def start(r):
        for k in range(TOP_K):
            row_copy(r, k).start(priority=k)

    def wait(r):
        for k in range(TOP_K):
            row_copy(r, k).wait()

    _for_rows(TM, start)
    _for_rows(TM, wait)


def _moe_dispatch(h2, pos, fill_blk, n_used, n_blocks):
    s = h2.shape[0]
    grid_spec = pltpu.PrefetchScalarGridSpec(
        num_scalar_prefetch=2,
        grid=(s // TM,),
        in_specs=[pl.BlockSpec((1, 1, TOP_K * TM), lambda i, f, u: (i, 0, 0), memory_space=pltpu.SMEM),
                  pl.BlockSpec((TM, D_MODEL), lambda i, f, u: (i, 0))],
        out_specs=pl.BlockSpec(memory_space=pl.ANY),
        scratch_shapes=[pltpu.VMEM((MOE_BLOCK, D_MODEL), F32), pltpu.SemaphoreType.DMA,
                        pltpu.SemaphoreType.DMA],
    )
    return pl.pallas_call(
        _dispatch_kernel,
        grid_spec=grid_spec,
        out_shape=jax.ShapeDtypeStruct((n_blocks * MOE_BLOCK, D_MODEL), F32),
        compiler_params=_cparams(("arbitrary",)),
        name="moe_dispatch",
    )(fill_blk, n_used, pos.reshape(s // TM, 1, TOP_K * TM), h2)


def _moe_kernel(blk_e, n_used, xs_ref, wg_ref, wu_ref, wd_ref, o_ref):
    b = pl.program_id(0)

    @pl.when(b < n_used[0])
    def _():
        o_ref[...] = _swiglu_chunks(xs_ref[...].astype(BF16), wg_ref.at[0, 0], wu_ref.at[0, 0], wd_ref.at[0, 0])

    @pl.when(b >= n_used[0])
    def _():
        o_ref[...] = jnp.zeros_like(o_ref)


def _moe_experts(xs, blk_e, n_used, wg, wu, wd, layer, n_blocks):
    weights = lambda a: pl.BlockSpec((1, 1) + a.shape[2:], lambda b, e, u: (layer, e[b], 0, 0),
                                     pipeline_mode=pl.Buffered(1))
    grid_spec = pltpu.PrefetchScalarGridSpec(
        num_scalar_prefetch=2,
        grid=(n_blocks,),
        in_specs=[pl.BlockSpec((MOE_BLOCK, D_MODEL), lambda b, e, u: (jnp.minimum(b, u[0] - 1), 0)),
                  weights(wg), weights(wu), weights(wd)],
        out_specs=pl.BlockSpec((MOE_BLOCK, D_MODEL), lambda b, e, u: (b, 0)),
    )
    return pl.pallas_call(
        _moe_kernel,
        grid_spec=grid_spec,
        out_shape=jax.ShapeDtypeStruct((n_blocks * MOE_BLOCK, D_MODEL), F32),
        compiler_params=_cparams(("arbitrary",)),
        name="moe_experts",
    )(blk_e, n_used, xs, wg, wu, wd)


def _combine_kernel(pos_ref, x_ref, gate_ref, ys_hbm, o_ref, ybuf, sem):
    def row_copy(r, k):
        return pltpu.make_async_copy(ys_hbm.at[pl.ds(pos_ref[0, 0, TOP_K * r + k], 1)],
                                     ybuf.at[k, pl.ds(r, 1)], sem)

    def start(r):
        for k in range(TOP_K):
            row_copy(r, k).start(priority=k)

    def wait(r):
        for k in range(TOP_K):
            row_copy(r, k).wait()

    _for_rows(TM, start)
    _for_rows(TM, wait)
    gates = gate_ref[...]
    gated = [ybuf[k] * gates[:, k:k + 1] for k in range(TOP_K)]
    o_ref[...] = x_ref[...] + functools.reduce(lambda a, b_: a + b_, gated)


def _moe_combine(x1, gates, pos, ys):
    s = x1.shape[0]
    row = lambda w_: pl.BlockSpec((TM, w_), lambda i: (i, 0))
    return pl.pallas_call(
        _combine_kernel,
        grid=(s // TM,),
        in_specs=[pl.BlockSpec((1, 1, TOP_K * TM), lambda i: (i, 0, 0), memory_space=pltpu.SMEM),
                  row(D_MODEL), row(TOP_K), pl.BlockSpec(memory_space=pl.ANY)],
        out_specs=row(D_MODEL),
        out_shape=jax.ShapeDtypeStruct((s, D_MODEL), F32),
        scratch_shapes=[pltpu.VMEM((TOP_K, TM, D_MODEL), F32), pltpu.SemaphoreType.DMA],
        compiler_params=_cparams(("arbitrary",)),
        name="moe_combine",
    )(pos.reshape(s // TM, 1, TOP_K * TM), x1, gates, ys)


def _moe_layer(x1, h2, route, wg, wu, wd, layer):
    s = h2.shape[0]
    n_blocks = s * TOP_K // MOE_BLOCK + N_EXPERTS
    flat_e = route[:TOP_K].T.astype(jnp.int32).reshape(-1)
    gates = route[TOP_K:2 * TOP_K].T
    onehot = (flat_e[:, None] == jnp.arange(N_EXPERTS)[None, :]).astype(jnp.int32)
    csum = jnp.cumsum(onehot, axis=0)
    counts = csum[-1]
    padded = (counts + MOE_BLOCK - 1) // MOE_BLOCK * MOE_BLOCK
    pend = jnp.cumsum(padded)
    pstart = pend - padded
    pos = jnp.sum(onehot * (csum - 1 + pstart[None, :]), axis=1).astype(jnp.int32)
    blk_start = jnp.arange(n_blocks, dtype=jnp.int32) * MOE_BLOCK
    blk_e = jnp.minimum(jnp.sum((pend[None, :] <= blk_start[:, None]).astype(jnp.int32), axis=1),
                        N_EXPERTS - 1).astype(jnp.int32)
    n_used = (pend[-1] // MOE_BLOCK).astype(jnp.int32).reshape(1)
    fill_blk = jnp.maximum(pend // MOE_BLOCK - 1, 0).astype(jnp.int32)
    xs = _moe_dispatch(h2, pos, fill_blk, n_used, n_blocks)
    ys = _moe_experts(xs, blk_e, n_used, wg, wu, wd, layer, n_blocks)
    return _moe_combine(x1, gates, pos, ys)


def _block_diag_ones(width, group):
    idx = jnp.arange(width) // group
    return (idx[:, None] == idx[None, :]).astype(BF16)


def kernel(x, norm_mix, w_in, q_norm_sb, k_norm_sb, q_norm_ca, k_norm_ca, rel_bias, lam_re, lam_im, log_dt, b_re, b_im, c_re, c_im, d_skip, w_glu, g_out, w_out, norm_ffn, w_gate_dense, w_up_dense, w_down_dense, w_router, w_gate_moe, w_up_moe, w_down_moe):
    b, s, d = x.shape
    assert b == 1 and d == D_MODEL and s % (TM * 2) == 0
    xs = x.reshape(s, d).astype(F32)
    scale = 1.0 / math.sqrt(HEAD_DIM)
    gm = _block_diag_ones(GROUP_SUM_WIDTH, HEAD_DIM)
    idx = jnp.arange(SB_BLOCK)
    tri = (idx[:, None] > idx[None, :]).astype(BF16)
    vec = lambda a: a.reshape(1, -1).astype(F32)
    tile = lambda a, width: jnp.tile(a.astype(F32), width // HEAD_DIM).reshape(1, width)

    dense_w = [w.astype(F32) for w in (w_gate_dense, w_up_dense, w_down_dense)]
    moe_w = [w.astype(F32) for w in (w_gate_moe, w_up_moe, w_down_moe)]
    ca_bias = _ca_bias_mask(rel_bias.reshape(DEPTH * CA_HEADS, 2 * REL_CLIP + 1))

    for l in range(DEPTH):
        q_sb, k_sb, v_sb, q_ca, k_ca, v_ca, u = _in_proj(
            xs, vec(norm_mix[l]), w_in[l].astype(BF16),
            tile(q_norm_sb[l], SB_WIDTH) * (scale * LOG2E), tile(k_norm_sb[l], SB_WIDTH),
            tile(q_norm_ca[l], CA_WIDTH) * scale, tile(k_norm_ca[l], CA_WIDTH), gm)
        o_sb = _sb_attention(q_sb, k_sb, v_sb, tri)
        o_ca = _ca_attention(q_ca, k_ca, v_ca, ca_bias, layer=l)
        o_ssm = _s5_mixer(u, lam_re[l], lam_im[l], log_dt[l], b_re[l], b_im[l], c_re[l], c_im[l],
                          d_skip[l], w_glu[l])
        i = l // 2
        if l % 2 == 0:
            x1, h2 = _mix_out(o_sb, o_ca, o_ssm, xs, vec(g_out[l]), w_out[l].astype(BF16), vec(norm_ffn[l]))
            xs = _ffn_dense(h2, x1, *dense_w, layer=i)
        else:
            r_hi, r_lo = _split_bf16(w_router[i].astype(F32).T)
            r_both = jnp.concatenate([r_hi, r_lo], axis=0)
            r_hi = jnp.concatenate([r_hi, jnp.zeros_like(r_hi)], axis=0)
            x1, h2, route = _mix_out(o_sb, o_ca, o_ssm, xs, vec(g_out[l]), w_out[l].astype(BF16),
                                     vec(norm_ffn[l]), router=(r_both, r_hi))
            xs = _moe_layer(x1, h2, route, *moe_w, layer=i)
    return xs.reshape(b, s, d).astype(x.dtype)
```
